```python
import math
import jax, jax.numpy as jnp
from jax import lax
import numpy as np

D_MODEL = 1024
BATCH = 8
SEQ = 2048
DEPTH = 2

HEAD_DIM = 64
FOX_HEADS = 8
SB_HEADS = 8
DIFF_HEADS = 4
DIFF_VDIM = 2 * HEAD_DIM
FOX_WIDTH = FOX_HEADS * HEAD_DIM
SB_WIDTH = SB_HEADS * HEAD_DIM
DIFF_QK_WIDTH = DIFF_HEADS * 2 * HEAD_DIM
DIFF_V_WIDTH = DIFF_HEADS * DIFF_VDIM
BRANCH_WIDTH = 512
N_BRANCHES = 3
Q_BLOCK = 128
FFN_DIM = 3584
N_EXPERTS = 8
TOP_K = 2
RMS_EPS = 1e-6
ALIBI_MAX_BIAS = 8.0
N_DENSE = (DEPTH + 1) // 2
N_MOE = DEPTH // 2
SPLIT_SIZES = (FOX_WIDTH, FOX_WIDTH, FOX_WIDTH, FOX_HEADS,
               SB_WIDTH, SB_WIDTH, SB_WIDTH,
               DIFF_QK_WIDTH, DIFF_QK_WIDTH, DIFF_V_WIDTH,
               N_BRANCHES * D_MODEL)
IN_DIM = sum(SPLIT_SIZES)

kernel_name = 'hybrid_fox_stickbreak_diffattn_moe_block'


def rmsnorm(x, g):
    xf = x.astype(jnp.float32)
    ms = jnp.mean(xf * xf, axis=-1, keepdims=True)
    return (xf * lax.rsqrt(ms + RMS_EPS) * g.astype(jnp.float32)).astype(x.dtype)


def to_heads(t, n_heads):
    b, s, _ = t.shape
    return t.reshape(b, s, n_heads, -1).transpose(0, 2, 1, 3)


def from_heads(t):
    b, h, s, d = t.shape
    return t.transpose(0, 2, 1, 3).reshape(b, s, h * d)


def block_positions(t0, t1):
    t_idx = t0 + jnp.arange(t1 - t0)[:, None]
    s_idx = jnp.arange(t1)[None, :]
    return t_idx, s_idx


def fox_attention(q, k, v, log_f):
    seq = q.shape[2]
    cum_f = jnp.cumsum(log_f, axis=-1)
    scale = HEAD_DIM ** -0.5
    outs = []
    for blk in range(seq // Q_BLOCK):
        t0, t1 = blk * Q_BLOCK, (blk + 1) * Q_BLOCK
        t_idx, s_idx = block_positions(t0, t1)
        logits = jnp.einsum('bhtd,bhsd->bhts', q[:, :, t0:t1], k[:, :, :t1]).astype(jnp.float32) * scale
        logits = logits + cum_f[:, :, t0:t1, None] - cum_f[:, :, None, :t1]
        logits = jnp.where(s_idx <= t_idx, logits, -jnp.inf)
        p = jax.nn.softmax(logits, axis=-1)
        outs.append(jnp.einsum('bhts,bhsd->bhtd', p.astype(v.dtype), v[:, :, :t1]))
    return jnp.concatenate(outs, axis=2)


def stick_breaking_attention(q, k, v):
    seq = q.shape[2]
    scale = HEAD_DIM ** -0.5
    outs = []
    for blk in range(seq // Q_BLOCK):
        t0, t1 = blk * Q_BLOCK, (blk + 1) * Q_BLOCK
        t_idx, s_idx = block_positions(t0, t1)
        z = jnp.einsum('bhtd,bhsd->bhts', q[:, :, t0:t1], k[:, :, :t1]).astype(jnp.float32) * scale
        mask = s_idx < t_idx
        log_stay = jnp.where(mask, jax.nn.log_sigmoid(-z), 0.0)
        tail = lax.cumsum(log_stay, axis=3, reverse=True) - log_stay
        log_a = jax.nn.log_sigmoid(z) + tail
        a = jnp.where(mask, jnp.exp(log_a), 0.0)
        outs.append(jnp.einsum('bhts,bhsd->bhtd', a.astype(v.dtype), v[:, :, :t1]))
    return jnp.concatenate(outs, axis=2)


def alibi_slopes(n_heads):
    h = jnp.arange(1, n_heads + 1, dtype=jnp.float32)
    return jnp.exp2(-ALIBI_MAX_BIAS * h / n_heads)


def diff_attention(q, k, v, lam):
    seq = q.shape[2]
    scale = HEAD_DIM ** -0.5
    q1, q2 = q[..., :HEAD_DIM], q[..., HEAD_DIM:]
    k1, k2 = k[..., :HEAD_DIM], k[..., HEAD_DIM:]
    slopes = alibi_slopes(q.shape[1])[:, None, None]
    outs = []
    for blk in range(seq // Q_BLOCK):
        t0, t1 = blk * Q_BLOCK, (blk + 1) * Q_BLOCK
        t_idx, s_idx = block_positions(t0, t1)
        mask = s_idx <= t_idx
        bias = -slopes * (t_idx - s_idx).astype(jnp.float32)
        l1 = jnp.einsum('bhtd,bhsd->bhts', q1[:, :, t0:t1], k1[:, :, :t1]).astype(jnp.float32) * scale + bias
        l2 = jnp.einsum('bhtd,bhsd->bhts', q2[:, :, t0:t1], k2[:, :, :t1]).astype(jnp.float32) * scale + bias
        p1 = jax.nn.softmax(jnp.where(mask, l1, -jnp.inf), axis=-1)
        p2 = jax.nn.softmax(jnp.where(mask, l2, -jnp.inf), axis=-1)
        p = p1 - lam * p2
        outs.append(jnp.einsum('bhts,bhsd->bhtd', p.astype(v.dtype), v[:, :, :t1]))
    return jnp.concatenate(outs, axis=2)


def hybrid_mixer(h, w_in, fox_f_bias, diff_lambda, diff_subln_g, w_branch, w_out, lam_init):
    proj = h @ w_in
    offsets = np.cumsum(SPLIT_SIZES)[:-1].tolist()
    fq, fk, fv, ff, sq, sk, sv, dq, dk, dv, gl = jnp.split(proj, offsets, axis=-1)
    log_f = jax.nn.log_sigmoid(ff.astype(jnp.float32) + fox_f_bias.astype(jnp.float32)).transpose(0, 2, 1)
    o_fox = from_heads(fox_attention(to_heads(fq, FOX_HEADS), to_heads(fk, FOX_HEADS),
                                     to_heads(fv, FOX_HEADS), log_f))
    o_sb = from_heads(stick_breaking_attention(to_heads(sq, SB_HEADS), to_heads(sk, SB_HEADS),
                                               to_heads(sv, SB_HEADS)))
    dl = diff_lambda.astype(jnp.float32)
    lam = jnp.exp(jnp.sum(dl[0] * dl[1])) - jnp.exp(jnp.sum(dl[2] * dl[3])) + lam_init
    o_d = diff_attention(to_heads(dq, DIFF_HEADS), to_heads(dk, DIFF_HEADS), to_heads(dv, DIFF_HEADS), lam)
    o_diff = from_heads(rmsnorm(o_d, diff_subln_g) * (1.0 - lam_init))
    branches = jnp.stack([o_fox, o_sb, o_diff], axis=2)
    projected = jnp.einsum('bsnc,ncd->bsnd', branches, w_branch)
    gates = jax.nn.sigmoid(gl.reshape(gl.shape[0], gl.shape[1], N_BRANCHES, D_MODEL))
    merged = jnp.sum(gates * projected, axis=2)
    return merged @ w_out


def swiglu(h, w_gate, w_up, w_down):
    return (jax.nn.silu(h @ w_gate) * (h @ w_up)) @ w_down


def moe_swiglu(h, router, w_gate, w_up, w_down):
    logits = (h @ router).astype(jnp.float32)
    top_vals, top_idx = lax.top_k(logits, TOP_K)
    weights = jax.nn.softmax(top_vals, axis=-1)
    combine = jnp.sum(jax.nn.one_hot(top_idx, N_EXPERTS, dtype=jnp.float32) * weights[..., None], axis=-2)
    out = jnp.zeros_like(h)
    for e in range(N_EXPERTS):
        out = out + combine[..., e:e + 1].astype(h.dtype) * swiglu(h, w_gate[e], w_up[e], w_down[e])
    return out


def ada_modulation(c, w, b):
    mod = jax.nn.silu(c) @ w + b
    shift, scale, gate = jnp.split(mod, 3, axis=-1)
    return shift[:, None, :], scale[:, None, :], gate[:, None, :]


def setup_inputs(seed: int = 0) -> dict:
    key = jax.random.key(seed)
    ks = jax.random.split(key, 20)
    f32 = jnp.float32
    def nrm(k, shape, s):
        return jax.random.normal(k, shape, f32) * s
    return {
        'x': nrm(ks[0], (BATCH, SEQ, D_MODEL), 1.0),
        'c': nrm(ks[1], (BATCH, D_MODEL), 1.0),
        'w_ada': nrm(ks[2], (DEPTH, 2, D_MODEL, 3 * D_MODEL), D_MODEL ** -0.5),
        'b_ada': nrm(ks[3], (DEPTH, 2, 3 * D_MODEL), 0.01),
        'norm_g': 1.0 + nrm(ks[4], (DEPTH, 4, D_MODEL), 0.02),
        'w_in': nrm(ks[5], (DEPTH, D_MODEL, IN_DIM), D_MODEL ** -0.5),
        'fox_f_bias': 2.0 + nrm(ks[6], (DEPTH, FOX_HEADS), 0.5),
        'diff_lambda': nrm(ks[7], (DEPTH, 4, HEAD_DIM), 0.1),
        'diff_subln_g': 1.0 + nrm(ks[8], (DEPTH, DIFF_VDIM), 0.02),
        'w_branch': nrm(ks[9], (DEPTH, N_BRANCHES, BRANCH_WIDTH, D_MODEL), BRANCH_WIDTH ** -0.5),
        'w_out': nrm(ks[10], (DEPTH, D_MODEL, D_MODEL), D_MODEL ** -0.5),
        'ffn_w_gate': nrm(ks[11], (N_DENSE, D_MODEL, FFN_DIM), D_MODEL ** -0.5),
        'ffn_w_up': nrm(ks[12], (N_DENSE, D_MODEL, FFN_DIM), D_MODEL ** -0.5),
        'ffn_w_down': nrm(ks[13], (N_DENSE, FFN_DIM, D_MODEL), FFN_DIM ** -0.5),
        'moe_router': nrm(ks[14], (N_MOE, D_MODEL, N_EXPERTS), D_MODEL ** -0.5),
        'moe_w_gate': nrm(ks[15], (N_MOE, N_EXPERTS, D_MODEL, FFN_DIM), D_MODEL ** -0.5),
        'moe_w_up': nrm(ks[16], (N_MOE, N_EXPERTS, D_MODEL, FFN_DIM), D_MODEL ** -0.5),
        'moe_w_down': nrm(ks[17], (N_MOE, N_EXPERTS, FFN_DIM, D_MODEL), FFN_DIM ** -0.5),
    }


def reference(x, c, w_ada, b_ada, norm_g, w_in, fox_f_bias, diff_lambda, diff_subln_g,
              w_branch, w_out, ffn_w_gate, ffn_w_up, ffn_w_down,
              moe_router, moe_w_gate, moe_w_up, moe_w_down):
    for layer in range(DEPTH):
        lam_init = 0.8 - 0.6 * math.exp(-0.3 * layer)
        shift, scale, gate = ada_modulation(c, w_ada[layer, 0], b_ada[layer, 0])
        h = rmsnorm(x, norm_g[layer, 0]) * (1.0 + scale) + shift
        y = hybrid_mixer(h, w_in[layer], fox_f_bias[layer], diff_lambda[layer], diff_subln_g[layer],
                         w_branch[layer], w_out[layer], lam_init)
        x = x + gate * rmsnorm(y, norm_g[layer, 1])
        shift, scale, gate = ada_modulation(c, w_ada[layer, 1], b_ada[layer, 1])
        h = rmsnorm(x, norm_g[layer, 2]) * (1.0 + scale) + shift
        if layer % 2 == 0:
            i = layer // 2
            y = swiglu(h, ffn_w_gate[i], ffn_w_up[i], ffn_w_down[i])
        else:
            i = layer // 2
            y = moe_swiglu(h, moe_router[i], moe_w_gate[i], moe_w_up[i], moe_w_down[i])
        x = x + gate * rmsnorm(y, norm_g[layer, 3])
    return x
```

```python
import functools
import math

import jax
import jax.numpy as jnp
from jax import lax
from jax.experimental import pallas as pl
from jax.experimental.pallas import tpu as pltpu

F32 = jnp.float32
BF16 = jnp.bfloat16

D_MODEL = 1024
HEAD_DIM = 64
FOX_HEADS = 8
SB_HEADS = 8
DIFF_HEADS = 4
BRANCH_WIDTH = 512
N_BRANCHES = 3
FFN_DIM = 3584
N_EXPERTS = 8
RMS_EPS = 1e-6
ALIBI_MAX_BIAS = 8.0

LANES = 128
GATE_COLS = N_BRANCHES * D_MODEL
COLBLK_FQ = GATE_COLS // LANES
COLBLK_FK = COLBLK_FQ + 4
COLBLK_FV = COLBLK_FK + 4
COLBLK_SQ = COLBLK_FV + 4
COLBLK_SK = COLBLK_SQ + 4
COLBLK_SV = COLBLK_SK + 4
COLBLK_DQ = COLBLK_SV + 4
COLBLK_DK = COLBLK_DQ + 4
COLBLK_DV = COLBLK_DK + 4
PROJ_COLS = (COLBLK_DV + 4) * LANES

VMEM_LIMIT = 56 * 1024 * 1024
NEG_BIG = -1e30
QK_SCALE = HEAD_DIM ** -0.5


def _cparams(sem):
    return pltpu.CompilerParams(dimension_semantics=sem, vmem_limit_bytes=VMEM_LIMIT)


def _sigmoid(v):
    return 1.0 / (1.0 + jnp.exp(-v))


def _rms(v, g):
    ms = jnp.mean(v * v, axis=-1, keepdims=True)
    return v * lax.rsqrt(ms + RMS_EPS) * g


def _ada_kernel(c_ref, w_ref, b_ref, o_ref):
    c = c_ref[...]
    a = (c * _sigmoid(c)).astype(BF16)
    o_ref[...] = jnp.dot(a, w_ref[...].astype(BF16), preferred_element_type=F32) + b_ref[...]


def _ada_modulation(c, w_ada, b_ada):
    n_mod = w_ada.shape[0] * w_ada.shape[1]
    batch = c.shape[0]
    w = w_ada.reshape(n_mod, D_MODEL, 3 * D_MODEL)
    b = b_ada.reshape(n_mod, 1, 3 * D_MODEL)
    out = pl.pallas_call(
        _ada_kernel,
        grid=(n_mod, 3),
        in_specs=[
            pl.BlockSpec((batch, D_MODEL), lambda m, j: (0, 0)),
            pl.BlockSpec((None, D_MODEL, D_MODEL), lambda m, j: (m, 0, j)),
            pl.BlockSpec((None, 1, D_MODEL), lambda m, j: (m, 0, j)),
        ],
        out_specs=pl.BlockSpec((None, batch, D_MODEL), lambda m, j: (m, 0, j)),
        out_shape=jax.ShapeDtypeStruct((n_mod, batch, 3 * D_MODEL), F32),
        compiler_params=_cparams(("parallel", "parallel")),
        name="ada_modulation",
    )(c, w, b)
    return out.reshape(n_mod, batch, 1, 3 * D_MODEL)


def _mod_spec(mod_idx, part, rows_per_batch_tiles):
    def index(i, *_):
        return (mod_idx, i // rows_per_batch_tiles, 0, part)
    return pl.BlockSpec((None, None, 1, D_MODEL), index)


def _inproj_kernel(x_ref, g_ref, sc_ref, sh_ref, w_ref, wf_ref, o_ref, f_ref, h_scr):
    @pl.when(pl.program_id(1) == 0)
    def _():
        h = _rms(x_ref[...], g_ref[...]) * (1.0 + sc_ref[...]) + sh_ref[...]
        hb = h.astype(BF16)
        h_scr[...] = hb
        f_ref[...] = jnp.dot(hb, wf_ref[...], preferred_element_type=F32)

    o_ref[...] = jnp.dot(h_scr[...], w_ref[...], preferred_element_type=F32).astype(BF16)


def _in_projection(x2d, g, mod, mod_idx, w_main, w_forget, seq):
    tokens = x2d.shape[0]
    tm = min(1024, seq)
    tn = 1536
    tiles_per_batch = seq // tm
    return pl.pallas_call(
        _inproj_kernel,
        grid=(tokens // tm, PROJ_COLS // tn),
        in_specs=[
            pl.BlockSpec((tm, D_MODEL), lambda i, j: (i, 0)),
            pl.BlockSpec((1, D_MODEL), lambda i, j: (0, 0)),
            _mod_spec(mod_idx, 1, tiles_per_batch),
            _mod_spec(mod_idx, 0, tiles_per_batch),
            pl.BlockSpec((D_MODEL, tn), lambda i, j: (0, j)),
            pl.BlockSpec((D_MODEL, LANES), lambda i, j: (0, 0)),
        ],
        out_specs=[
            pl.BlockSpec((tm, tn), lambda i, j: (i, j)),
            pl.BlockSpec((tm, LANES), lambda i, j: (i, 0)),
        ],
        out_shape=[
            jax.ShapeDtypeStruct((tokens, PROJ_COLS), BF16),
            jax.ShapeDtypeStruct((tokens, LANES), F32),
        ],
        scratch_shapes=[pltpu.VMEM((tm, D_MODEL), BF16)],
        compiler_params=_cparams(("parallel", "arbitrary")),
        name="in_projection",
    )(x2d, g, mod, mod, w_main, w_forget)


CUMSUM_BLOCK = 256


def _split3(v):
    p1 = v.astype(BF16)
    r1 = v - p1.astype(F32)
    p2 = r1.astype(BF16)
    p3 = (r1 - p2.astype(F32)).astype(BF16)
    return p1, p2, p3


def _decay_kernel(f_ref, b_ref, o_ref):
    seq = f_ref.shape[0]
    row = lax.broadcasted_iota(jnp.int32, (CUMSUM_BLOCK, CUMSUM_BLOCK), 0)
    col = lax.broadcasted_iota(jnp.int32, (CUMSUM_BLOCK, CUMSUM_BLOCK), 1)
    tri = jnp.where(col <= row, 1.0, 0.0).astype(BF16)
    carry = jnp.zeros((1, LANES), F32)
    for blk in range(seq // CUMSUM_BLOCK):
        z = f_ref[blk * CUMSUM_BLOCK:(blk + 1) * CUMSUM_BLOCK, :] + b_ref[...]
        logf = jnp.minimum(z, 0.0) - jnp.log1p(jnp.exp(-jnp.abs(z)))
        cum = carry
        for piece in _split3(logf):
            cum = cum + jnp.dot(tri, piece, preferred_element_type=F32)
        carry = cum[CUMSUM_BLOCK - 1:CUMSUM_BLOCK, :]
        cum_t = cum.T
        o_ref[:, blk * CUMSUM_BLOCK:(blk + 1) * CUMSUM_BLOCK] = -cum_t[0:FOX_HEADS, :]


def _fox_decay(forget_logits, bias, batch, seq):
    bias_row = jnp.zeros((1, LANES), F32).at[0, :FOX_HEADS].set(bias.astype(F32))
    out = pl.pallas_call(
        _decay_kernel,
        grid=(batch,),
        in_specs=[
            pl.BlockSpec((seq, LANES), lambda b: (b, 0)),
            pl.BlockSpec((1, LANES), lambda b: (0, 0)),
        ],
        out_specs=pl.BlockSpec((None, FOX_HEADS, seq), lambda b: (b, 0, 0)),
        out_shape=jax.ShapeDtypeStruct((batch, FOX_HEADS, seq), F32),
        compiler_params=_cparams(("parallel",)),
        name="fox_decay",
    )(forget_logits, bias_row)
    return out.reshape(batch, FOX_HEADS // 2, 2, seq)


ATT_BLOCK = 256


def _qk(q, k):
    return lax.dot_general(q, k, (((1,), (1,)), ((), ())), preferred_element_type=F32)


def _lane_halves(q):
    low = lax.broadcasted_iota(jnp.int32, q.shape, 1) < HEAD_DIM
    zero = jnp.zeros_like(q)
    return jnp.where(low, q, zero), jnp.where(low, zero, q)


def _causal_masks(t):
    row = lax.broadcasted_iota(jnp.int32, (t, t), 0)
    col = lax.broadcasted_iota(jnp.int32, (t, t), 1)
    return col <= row, col < row


def _softmax_step(s, m, l, acc, v):
    m_new = jnp.maximum(m, jnp.max(s, axis=1, keepdims=True))
    alpha = jnp.exp(m - m_new)
    p = jnp.exp(s - m_new)
    l = alpha * l + jnp.sum(p, axis=1, keepdims=True)
    acc = alpha * acc + jnp.dot(p.astype(BF16), v, preferred_element_type=F32)
    return m_new, l, acc


def _softmax_init(t):
    return (jnp.full((t, 1), NEG_BIG, F32), jnp.zeros((t, 1), F32), jnp.zeros((t, LANES), F32))


def _fox_kernel(q_ref, k_ref, v_ref, nf_ref, o_ref):
    t = ATT_BLOCK
    n_blocks = q_ref.shape[0] // t
    incl, _ = _causal_masks(t)

    def q_block(qi, _):
        q0 = pl.multiple_of(qi * t, t)
        qa, qb = _lane_halves(q_ref[pl.ds(q0, t), :] * QK_SCALE)

        def kv_block(k0, carry, diagonal):
            k = k_ref[pl.ds(k0, t), :]
            v = v_ref[pl.ds(k0, t), :]
            out = []
            for h, qh in enumerate((qa, qb)):
                s = _qk(qh, k) + nf_ref[h:h + 1, pl.ds(k0, t)]
                if diagonal:
                    s = jnp.where(incl, s, NEG_BIG)
                out.append(_softmax_step(s, *carry[h], v))
            return tuple(out)

        carry = (_softmax_init(t), _softmax_init(t))
        carry = lax.fori_loop(
            0, qi, lambda j, c: kv_block(pl.multiple_of(j * t, t), c, False), carry)
        (_, la, acca), (_, lb, accb) = kv_block(q0, carry, True)
        low = lax.broadcasted_iota(jnp.int32, (t, LANES), 1) < HEAD_DIM
        o_ref[pl.ds(q0, t), :] = jnp.where(low, acca / la, accb / lb).astype(o_ref.dtype)
        return 0

    lax.fori_loop(0, n_blocks, q_block, 0)


def _sb_kernel(q_ref, k_ref, v_ref, o_ref):
    t = ATT_BLOCK
    n_blocks = q_ref.shape[0] // t
    _, strict = _causal_masks(t)
    row = lax.broadcasted_iota(jnp.int32, (t, t), 0)
    col = lax.broadcasted_iota(jnp.int32, (t, t), 1)
    suffix = jnp.where(row > col, 1.0, 0.0).astype(BF16)

    def q_block(qi, _):
        q0 = pl.multiple_of(qi * t, t)
        qa, qb = _lane_halves(q_ref[pl.ds(q0, t), :] * QK_SCALE)

        def kv_block(k0, carry, diagonal):
            k = k_ref[pl.ds(k0, t), :]
            v = v_ref[pl.ds(k0, t), :]
            out = []
            for h, qh in enumerate((qa, qb)):
                later, acc = carry[h]
                z = _qk(qh, k)
                softplus = jnp.maximum(z, 0.0) + jnp.log1p(jnp.exp(-jnp.abs(z)))
                log_stay = -softplus
                if diagonal:
                    log_stay = jnp.where(strict, log_stay, 0.0)
                tail = jnp.dot(log_stay.astype(BF16), suffix, preferred_element_type=F32) + later
                a = jnp.exp(z + log_stay + tail)
                if diagonal:
                    a = jnp.where(strict, a, 0.0)
                acc = acc + jnp.dot(a.astype(BF16), v, preferred_element_type=F32)
                later = later + jnp.sum(log_stay, axis=1, keepdims=True)
                out.append((later, acc))
            return tuple(out)

        init = (jnp.zeros((t, 1), F32), jnp.zeros((t, LANES), F32))
        carry = kv_block(q0, (init, init), True)
        carry = lax.fori_loop(
            0, qi, lambda j, c: kv_block(pl.multiple_of((qi - 1 - j) * t, t), c, False), carry)
        (_, acca), (_, accb) = carry
        low = lax.broadcasted_iota(jnp.int32, (t, LANES), 1) < HEAD_DIM
        o_ref[pl.ds(q0, t), :] = jnp.where(low, acca, accb).astype(o_ref.dtype)
        return 0

    lax.fori_loop(0, n_blocks, q_block, 0)


def _diff_kernel(q_ref, k_ref, v_ref, slope_ref, dl_ref, g_ref, o_ref, *, lam_init):
    t = ATT_BLOCK
    n_blocks = q_ref.shape[0] // t
    incl, _ = _causal_masks(t)
    dl = dl_ref[...]
    lam = (jnp.exp(jnp.sum(dl[0:1] * dl[1:2], axis=1, keepdims=True))
           - jnp.exp(jnp.sum(dl[2:3] * dl[3:4], axis=1, keepdims=True)) + lam_init)
    key_pos = lax.broadcasted_iota(jnp.int32, (1, t), 1)
    slope = slope_ref[:, 0:1]

    def q_block(qi, _):
        q0 = pl.multiple_of(qi * t, t)
        qa, qb = _lane_halves(q_ref[pl.ds(q0, t), :] * QK_SCALE)

        def kv_block(k0, carry, diagonal):
            k = k_ref[pl.ds(k0, t), :]
            v = v_ref[pl.ds(k0, t), :]
            bias = slope * (key_pos + (k0 - q0)).astype(F32)
            out = []
            for h, qh in enumerate((qa, qb)):
                s = _qk(qh, k) + bias
                if diagonal:
                    s = jnp.where(incl, s, NEG_BIG)
                out.append(_softmax_step(s, *carry[h], v))
            return tuple(out)

        carry = (_softmax_init(t), _softmax_init(t))
        carry = lax.fori_loop(
            0, qi, lambda j, c: kv_block(pl.multiple_of(j * t, t), c, False), carry)
        (_, l1, acc1), (_, l2, acc2) = kv_block(q0, carry, True)
        o = acc1 / l1 - lam * (acc2 / l2)
        o_ref[pl.ds(q0, t), :] = (_rms(o, g_ref[...]) * (1.0 - lam_init)).astype(o_ref.dtype)
        return 0

    lax.fori_loop(0, n_blocks, q_block, 0)


def _attention_call(kernel, proj, colblks, extra_in, extra_specs, batch, seq, n_colblk, name):
    qc, kc, vc = colblks
    return pl.pallas_call(
        kernel,
        grid=(batch, n_colblk),
        in_specs=[
            pl.BlockSpec((seq, LANES), lambda b, c: (b, qc + c)),
            pl.BlockSpec((seq, LANES), lambda b, c: (b, kc + c)),
            pl.BlockSpec((seq, LANES), lambda b, c: (b, vc + c)),
        ] + extra_specs,
        out_specs=pl.BlockSpec((seq, LANES), lambda b, c: (b, c)),
        out_shape=jax.ShapeDtypeStruct((batch * seq, n_colblk * LANES), BF16),
        compiler_params=_cparams(("parallel", "parallel")),
        name=name,
    )(proj, proj, proj, *extra_in)


def _router_combine(h, r_ref):
    h1 = h.astype(BF16)
    h2 = (h - h1.astype(F32)).astype(BF16)
    prod = (jnp.dot(h1, r_ref[...], preferred_element_type=F32)
            + jnp.dot(h2, r_ref[...], preferred_element_type=F32))
    lane = lax.broadcasted_iota(jnp.int32, prod.shape, 1)
    shifted1 = pltpu.roll(prod, LANES - N_EXPERTS, axis=1)
    shifted2 = pltpu.roll(prod, LANES - 2 * N_EXPERTS, axis=1)
    logits = jnp.where(lane < N_EXPERTS, prod + shifted1 + shifted2, NEG_BIG)
    v1 = jnp.max(logits, axis=1, keepdims=True)
    i1 = jnp.min(jnp.where(logits == v1, lane, LANES), axis=1, keepdims=True)
    rest = jnp.where(lane == i1, NEG_BIG, logits)
    v2 = jnp.max(rest, axis=1, keepdims=True)
    i2 = jnp.min(jnp.where(rest == v2, lane, LANES), axis=1, keepdims=True)
    e2 = jnp.exp(v2 - v1)
    w1 = 1.0 / (1.0 + e2)
    w2 = e2 / (1.0 + e2)
    return jnp.where(lane == i1, w1, 0.0) + jnp.where(lane == i2, w2, 0.0)


def _merge_kernel(*refs, with_router):
    if with_router:
        (of_ref, os_ref, od_ref, gl_ref, x_ref, wb_ref, wo_ref, g1_ref, gate_ref,
         g2_ref, sc_ref, sh_ref, r_ref, xo_ref, ho_ref, cw_ref) = refs
    else:
        (of_ref, os_ref, od_ref, gl_ref, x_ref, wb_ref, wo_ref, g1_ref, gate_ref,
         g2_ref, sc_ref, sh_ref, xo_ref, ho_ref) = refs
    merged = None
    for n, o_ref in enumerate((of_ref, os_ref, od_ref)):
        p = jnp.dot(o_ref[...], wb_ref[n], preferred_element_type=F32)
        gated = _sigmoid(gl_ref[:, n * D_MODEL:(n + 1) * D_MODEL].astype(F32)) * p
        merged = gated if merged is None else merged + gated
    y = jnp.dot(merged.astype(BF16), wo_ref[...], preferred_element_type=F32)
    xn = x_ref[...] + gate_ref[...] * _rms(y, g1_ref[...])
    xo_ref[...] = xn
    h = _rms(xn, g2_ref[...]) * (1.0 + sc_ref[...]) + sh_ref[...]
    ho_ref[...] = h.astype(BF16)
    if with_router:
        cw_ref[...] = _router_combine(h, r_ref)


def _merge(o_fox, o_sb, o_diff, proj, x2d, w_branch, w_out, g1, g2, mod, mod_attn, mod_ffn,
           router_packed, seq):
    tokens = x2d.shape[0]
    tm = min(512, seq)
    tpb = seq // tm
    with_router = router_packed is not None
    row = lambda i: (i, 0)
    const2 = lambda i: (0, 0)
    in_specs = [
        pl.BlockSpec((tm, BRANCH_WIDTH), row),
        pl.BlockSpec((tm, BRANCH_WIDTH), row),
        pl.BlockSpec((tm, BRANCH_WIDTH), row),
        pl.BlockSpec((tm, GATE_COLS), row),
        pl.BlockSpec((tm, D_MODEL), row),
        pl.BlockSpec((N_BRANCHES, BRANCH_WIDTH, D_MODEL), lambda i: (0, 0, 0)),
        pl.BlockSpec((D_MODEL, D_MODEL), const2),
        pl.BlockSpec((1, D_MODEL), const2),
        _mod_spec(mod_attn, 2, tpb),
        pl.BlockSpec((1, D_MODEL), const2),
        _mod_spec(mod_ffn, 1, tpb),
        _mod_spec(mod_ffn, 0, tpb),
    ]
    args = [o_fox, o_sb, o_diff, proj, x2d, w_branch, w_out, g1, mod, g2, mod, mod]
    out_specs = [pl.BlockSpec((tm, D_MODEL), row), pl.BlockSpec((tm, D_MODEL), row)]
    out_shape = [jax.ShapeDtypeStruct((tokens, D_MODEL), F32),
                 jax.ShapeDtypeStruct((tokens, D_MODEL), BF16)]
    if with_router:
        in_specs.append(pl.BlockSpec((D_MODEL, LANES), const2))
        args.append(router_packed)
        out_specs.append(pl.BlockSpec((tm, LANES), row))
        out_shape.append(jax.ShapeDtypeStruct((tokens, LANES), F32))
    return pl.pallas_call(
        functools.partial(_merge_kernel, with_router=with_router),
        grid=(tokens // tm,),
        in_specs=in_specs,
        out_specs=out_specs,
        out_shape=out_shape,
        compiler_params=_cparams(("parallel",)),
        name="merge_router" if with_router else "merge",
    )(*args)


def _swiglu_partial(h, wg, wu):
    gate = jnp.dot(h, wg, preferred_element_type=F32)
    up = jnp.dot(h, wu, preferred_element_type=F32)
    return gate * _sigmoid(gate) * up


def _ffn_kernel(h_ref, wg_ref, wu_ref, wd_ref, x_ref, g_ref, gate_ref, o_ref, acc_ref):
    j = pl.program_id(1)

    @pl.when(j == 0)
    def _():
        acc_ref[...] = jnp.zeros_like(acc_ref)

    act = _swiglu_partial(h_ref[...], wg_ref[...], wu_ref[...])
    acc_ref[...] += jnp.dot(act.astype(BF16), wd_ref[...], preferred_element_type=F32)

    @pl.when(j == pl.num_programs(1) - 1)
    def _():
        o_ref[...] = x_ref[...] + gate_ref[...] * _rms(acc_ref[...], g_ref[...])


def _dense_ffn(h, x2d, wg, wu, wd, g, mod, mod_idx, seq):
    tokens = x2d.shape[0]
    tm = min(1024, seq)
    tf = 512
    tpb = seq // tm
    return pl.pallas_call(
        _ffn_kernel,
        grid=(tokens // tm, FFN_DIM // tf),
        in_specs=[
            pl.BlockSpec((tm, D_MODEL), lambda i, j: (i, 0)),
            pl.BlockSpec((D_MODEL, tf), lambda i, j: (0, j)),
            pl.BlockSpec((D_MODEL, tf), lambda i, j: (0, j)),
            pl.BlockSpec((tf, D_MODEL), lambda i, j: (j, 0)),
            pl.BlockSpec((tm, D_MODEL), lambda i, j: (i, 0)),
            pl.BlockSpec((1, D_MODEL), lambda i, j: (0, 0)),
            _mod_spec(mod_idx, 2, tpb),
        ],
        out_specs=pl.BlockSpec((tm, D_MODEL), lambda i, j: (i, 0)),
        out_shape=jax.ShapeDtypeStruct((tokens, D_MODEL), F32),
        scratch_shapes=[pltpu.VMEM((tm, D_MODEL), F32)],
        compiler_params=_cparams(("parallel", "arbitrary")),
        name="dense_ffn",
    )(h, wg, wu, wd, x2d, g, mod)


def _moe_kernel(h_ref, cw_ref, wg_ref, wu_ref, wd_ref, x_ref, g_ref, gate_ref, o_ref, acc_ref):
    e = pl.program_id(1)
    j = pl.program_id(2)

    @pl.when((e == 0) & (j == 0))
    def _():
        acc_ref[...] = jnp.zeros_like(acc_ref)

    cw = cw_ref[...]
    lane = lax.broadcasted_iota(jnp.int32, cw.shape, 1)
    weight = jnp.sum(jnp.where(lane == e, cw, 0.0), axis=1, keepdims=True)
    act = _swiglu_partial(h_ref[...], wg_ref[...], wu_ref[...]) * weight
    acc_ref[...] += jnp.dot(act.astype(BF16), wd_ref[...], preferred_element_type=F32)

    @pl.when((e == pl.num_programs(1) - 1) & (j == pl.num_programs(2) - 1))
    def _():
        o_ref[...] = x_ref[...] + gate_ref[...] * _rms(acc_ref[...], g_ref[...])


def _moe_ffn(h, combine, x2d, wg, wu, wd, g, mod, mod_idx, seq):
    tokens = x2d.shape[0]
    tm = min(1024, seq)
    tf = 512
    tpb = seq // tm
    return pl.pallas_call(
        _moe_kernel,
        grid=(tokens // tm, N_EXPERTS, FFN_DIM // tf),
        in_specs=[
            pl.BlockSpec((tm, D_MODEL), lambda i, e, j: (i, 0)),
            pl.BlockSpec((tm, LANES), lambda i, e, j: (i, 0)),
            pl.BlockSpec((None, D_MODEL, tf), lambda i, e, j: (e, 0, j)),
            pl.BlockSpec((None, D_MODEL, tf), lambda i, e, j: (e, 0, j)),
            pl.BlockSpec((None, tf, D_MODEL), lambda i, e, j: (e, j, 0)),
            pl.BlockSpec((tm, D_MODEL), lambda i, e, j: (i, 0)),
            pl.BlockSpec((1, D_MODEL), lambda i, e, j: (0, 0)),
            _mod_spec(mod_idx, 2, tpb),
        ],
        out_specs=pl.BlockSpec((tm, D_MODEL), lambda i, e, j: (i, 0)),
        out_shape=jax.ShapeDtypeStruct((tokens, D_MODEL), F32),
        scratch_shapes=[pltpu.VMEM((tm, D_MODEL), F32)],
        compiler_params=_cparams(("parallel", "arbitrary", "arbitrary")),
        name="moe_ffn",
    )(h, combine, wg, wu, wd, x2d, g, mod)


def _prep_w_in(w):
    fw = FOX_HEADS * HEAD_DIM
    sw = SB_HEADS * HEAD_DIM
    dw = DIFF_HEADS * 2 * HEAD_DIM
    sizes = (fw, fw, fw, FOX_HEADS, sw, sw, sw, dw, dw, dw, GATE_COLS)
    offs = [0]
    for s in sizes:
        offs.append(offs[-1] + s)
    part = [w[:, offs[i]:offs[i + 1]] for i in range(len(sizes))]
    fq, fk, fv, ff, sq, sk, sv, dq, dk, dv, gl = part
    w_main = jnp.concatenate([gl, fq, fk, fv, sq, sk, sv, dq, dk, dv], axis=1).astype(BF16)
    w_forget = jnp.pad(ff, ((0, 0), (0, LANES - FOX_HEADS))).astype(BF16)
    return w_main, w_forget


def _pack_router(r):
    p1 = r.astype(BF16)
    r1 = r - p1.astype(F32)
    p2 = r1.astype(BF16)
    p3 = (r1 - p2.astype(F32)).astype(BF16)
    packed = jnp.concatenate([p1, p2, p3], axis=1)
    return jnp.pad(packed, ((0, 0), (0, LANES - 3 * N_EXPERTS)))


def kernel(x, c, w_ada, b_ada, norm_g, w_in, fox_f_bias, diff_lambda, diff_subln_g, w_branch,
           w_out, ffn_w_gate, ffn_w_up, ffn_w_down, moe_router, moe_w_gate, moe_w_up,
           moe_w_down):
    batch, seq, _ = x.shape
    depth = w_in.shape[0]
    tokens = batch * seq
    x2d = x.reshape(tokens, D_MODEL)
    mod = _ada_modulation(c, w_ada, b_ada)
    slopes = jnp.exp2(-ALIBI_MAX_BIAS * jnp.arange(1, DIFF_HEADS + 1, dtype=F32) / DIFF_HEADS)
    slopes = jnp.broadcast_to(slopes[:, None, None], (DIFF_HEADS, 1, LANES))

    for layer in range(depth):
        lam_init = 0.8 - 0.6 * math.exp(-0.3 * layer)
        mod_attn, mod_ffn = 2 * layer, 2 * layer + 1
        w_main, w_forget = _prep_w_in(w_in[layer])
        proj, forget_logits = _in_projection(
            x2d, norm_g[layer, 0][None, :], mod, mod_attn, w_main, w_forget, seq)
        neg_decay = _fox_decay(forget_logits, fox_f_bias[layer], batch, seq)

        o_fox = _attention_call(
            _fox_kernel, proj, (COLBLK_FQ, COLBLK_FK, COLBLK_FV), [neg_decay],
            [pl.BlockSpec((None, None, 2, seq), lambda b, c: (b, c, 0, 0))],
            batch, seq, FOX_HEADS // 2, "fox_attention")
        o_sb = _attention_call(
            _sb_kernel, proj, (COLBLK_SQ, COLBLK_SK, COLBLK_SV), [], [],
            batch, seq, SB_HEADS // 2, "stickbreak_attention")
        o_diff = _attention_call(
            functools.partial(_diff_kernel, lam_init=lam_init), proj,
            (COLBLK_DQ, COLBLK_DK, COLBLK_DV),
            [slopes, diff_lambda[layer].astype(F32), diff_subln_g[layer][None, :].astype(F32)],
            [pl.BlockSpec((None, 1, LANES), lambda b, c: (c, 0, 0)),
             pl.BlockSpec((4, HEAD_DIM), lambda b, c: (0, 0)),
             pl.BlockSpec((1, 2 * HEAD_DIM), lambda b, c: (0, 0))],
            batch, seq, DIFF_HEADS, "diff_attention")

        is_moe = layer % 2 == 1
        idx = layer // 2
        router_packed = _pack_router(moe_router[idx]) if is_moe else None
        merged = _merge(
            o_fox, o_sb, o_diff, proj, x2d, w_branch[layer].astype(BF16),
            w_out[layer].astype(BF16), norm_g[layer, 1][None, :], norm_g[layer, 2][None, :],
            mod, mod_attn, mod_ffn, router_packed, seq)
        g3 = norm_g[layer, 3][None, :]
        if is_moe:
            x2d, h, combine = merged
            x2d = _moe_ffn(h, combine, x2d, moe_w_gate[idx].astype(BF16),
                           moe_w_up[idx].astype(BF16), moe_w_down[idx].astype(BF16),
                           g3, mod, mod_ffn, seq)
        else:
            x2d, h = merged
            x2d = _dense_ffn(h, x2d, ffn_w_gate[idx].astype(BF16), ffn_w_up[idx].astype(BF16),
                             ffn_w_down[idx].astype(BF16), g3, mod, mod_ffn, seq)
    return x2d.reshape(batch, seq, D_MODEL)
```

```python
import functools
import math

import jax
import jax.numpy as jnp
import numpy as np
from jax import lax
from jax.experimental import pallas as pl
from jax.experimental.pallas import tpu as pltpu

F32 = jnp.float32
BF16 = jnp.bfloat16

D_MODEL = 1024
HEAD_DIM = 64
FOX_HEADS = 8
SB_HEADS = 8
DIFF_HEADS = 4
BRANCH_WIDTH = 512
N_BRANCHES = 3
FFN_DIM = 3584
N_EXPERTS = 8
RMS_EPS = 1e-6
ALIBI_MAX_BIAS = 8.0

LANES = 128
GATE_COLS = N_BRANCHES * D_MODEL
COLBLK_FQ = GATE_COLS // LANES
COLBLK_FK = COLBLK_FQ + 4
COLBLK_FV = COLBLK_FK + 4
COLBLK_SQ = COLBLK_FV + 4
COLBLK_SK = COLBLK_SQ + 4
COLBLK_SV = COLBLK_SK + 4
COLBLK_DQ = COLBLK_SV + 4
COLBLK_DK = COLBLK_DQ + 4
COLBLK_DV = COLBLK_DK + 4
PROJ_COLS = (COLBLK_DV + 4) * LANES

VMEM_LIMIT = 56 * 1024 * 1024
NEG_BIG = -1e30
QK_SCALE = HEAD_DIM ** -0.5


def _cparams(sem):
    return pltpu.CompilerParams(dimension_semantics=sem, vmem_limit_bytes=VMEM_LIMIT)


def _sigmoid(v):
    return 1.0 / (1.0 + jnp.exp(-v))


def _rms(v, g):
    ms = jnp.mean(v * v, axis=-1, keepdims=True)
    return v * lax.rsqrt(ms + RMS_EPS) * g


def _ada_kernel(c_ref, w_ref, b_ref, o_ref):
    c = c_ref[...]
    a = (c * _sigmoid(c)).astype(BF16)
    o_ref[...] = jnp.dot(a, w_ref[...].astype(BF16), preferred_element_type=F32) + b_ref[...]


def _ada_modulation(c, w_ada, b_ada):
    n_mod = w_ada.shape[0] * w_ada.shape[1]
    batch = c.shape[0]
    w = w_ada.reshape(n_mod, D_MODEL, 3 * D_MODEL)
    b = b_ada.reshape(n_mod, 1, 3 * D_MODEL)
    out = pl.pallas_call(
        _ada_kernel,
        grid=(n_mod, 3),
        in_specs=[
            pl.BlockSpec((batch, D_MODEL), lambda m, j: (0, 0)),
            pl.BlockSpec((None, D_MODEL, D_MODEL), lambda m, j: (m, 0, j)),
            pl.BlockSpec((None, 1, D_MODEL), lambda m, j: (m, 0, j)),
        ],
        out_specs=pl.BlockSpec((None, batch, D_MODEL), lambda m, j: (m, 0, j)),
        out_shape=jax.ShapeDtypeStruct((n_mod, batch, 3 * D_MODEL), F32),
        compiler_params=_cparams(("parallel", "parallel")),
        name="ada_modulation",
    )(c, w, b)
    return out.reshape(n_mod, batch, 1, 3 * D_MODEL)


def _mod_spec(mod_idx, part, rows_per_batch_tiles):
    def index(i, *_):
        return (mod_idx, i // rows_per_batch_tiles, 0, part)
    return pl.BlockSpec((None, None, 1, D_MODEL), index)


def _inproj_kernel(x_ref, g_ref, sc_ref, sh_ref, w_ref, wf_ref, o_ref, f_ref, h_scr):
    @pl.when(pl.program_id(1) == 0)
    def _():
        h = _rms(x_ref[...], g_ref[...]) * (1.0 + sc_ref[...]) + sh_ref[...]
        hb = h.astype(BF16)
        h_scr[...] = hb
        f_ref[...] = jnp.dot(hb, wf_ref[...], preferred_element_type=F32)

    o_ref[...] = jnp.dot(h_scr[...], w_ref[...], preferred_element_type=F32).astype(BF16)


def _in_projection(x2d, g, mod, mod_idx, w_main, w_forget, seq):
    tokens = x2d.shape[0]
    tm = min(1024, seq)
    tn = 1536
    tiles_per_batch = seq // tm
    return pl.pallas_call(
        _inproj_kernel,
        grid=(tokens // tm, PROJ_COLS // tn),
        in_specs=[
            pl.BlockSpec((tm, D_MODEL), lambda i, j: (i, 0)),
            pl.BlockSpec((1, D_MODEL), lambda i, j: (0, 0)),
            _mod_spec(mod_idx, 1, tiles_per_batch),
            _mod_spec(mod_idx, 0, tiles_per_batch),
            pl.BlockSpec((D_MODEL, tn), lambda i, j: (0, j)),
            pl.BlockSpec((D_MODEL, LANES), lambda i, j: (0, 0)),
        ],
        out_specs=[
            pl.BlockSpec((tm, tn), lambda i, j: (i, j)),
            pl.BlockSpec((tm, LANES), lambda i, j: (i, 0)),
        ],
        out_shape=[
            jax.ShapeDtypeStruct((tokens, PROJ_COLS), BF16),
            jax.ShapeDtypeStruct((tokens, LANES), F32),
        ],
        scratch_shapes=[pltpu.VMEM((tm, D_MODEL), BF16)],
        compiler_params=_cparams(("parallel", "arbitrary")),
        name="in_projection",
    )(x2d, g, mod, mod, w_main, w_forget)


CUMSUM_BLOCK = 256
BIAS_PIECES = 3


def _split3(v):
    p1 = v.astype(BF16).astype(F32)
    r1 = v - p1
    p2 = r1.astype(BF16).astype(F32)
    p3 = (r1 - p2).astype(BF16).astype(F32)
    return p1, p2, p3


def _decay_kernel(f_ref, b_ref, sel_ref, o_ref):
    seq = f_ref.shape[0]
    row = lax.broadcasted_iota(jnp.int32, (CUMSUM_BLOCK, CUMSUM_BLOCK), 0)
    col = lax.broadcasted_iota(jnp.int32, (CUMSUM_BLOCK, CUMSUM_BLOCK), 1)
    tri = jnp.where(col <= row, 1.0, 0.0).astype(BF16)
    carry = jnp.zeros((1, LANES), F32)
    for blk in range(seq // CUMSUM_BLOCK):
        rows = slice(blk * CUMSUM_BLOCK, (blk + 1) * CUMSUM_BLOCK)
        z = f_ref[rows, :] + b_ref[...]
        logf = jnp.minimum(z, 0.0) - jnp.log1p(jnp.exp(-jnp.abs(z)))
        cum = carry
        for piece in _split3(logf):
            cum = cum + jnp.dot(tri, piece.astype(BF16), preferred_element_type=F32)
        carry = cum[CUMSUM_BLOCK - 1:CUMSUM_BLOCK, :]
        bias = None
        for i, piece in enumerate(_split3(-cum)):
            part = jnp.dot(piece.astype(BF16), sel_ref[i], preferred_element_type=F32)
            bias = part if bias is None else bias + part
        o_ref[rows, :] = bias.astype(BF16)


def _decay_selectors():
    sel = np.zeros((BIAS_PIECES, LANES, FOX_HEADS // 2 * LANES), np.float32)
    for h in range(FOX_HEADS):
        for i in range(BIAS_PIECES):
            sel[i, h, (h // 2) * LANES + (h % 2) * BIAS_PIECES + i] = 1.0
    return jnp.asarray(sel, BF16)


def _fox_decay(forget_logits, bias, batch, seq):
    bias_row = jnp.zeros((1, LANES), F32).at[0, :FOX_HEADS].set(bias.astype(F32))
    width = FOX_HEADS // 2 * LANES
    return pl.pallas_call(
        _decay_kernel,
        grid=(batch,),
        in_specs=[
            pl.BlockSpec((seq, LANES), lambda b: (b, 0)),
            pl.BlockSpec((1, LANES), lambda b: (0, 0)),
            pl.BlockSpec((BIAS_PIECES, LANES, width), lambda b: (0, 0, 0)),
        ],
        out_specs=pl.BlockSpec((seq, width), lambda b: (b, 0)),
        out_shape=jax.ShapeDtypeStruct((batch * seq, width), BF16),
        compiler_params=_cparams(("parallel",)),
        name="fox_decay",
    )(forget_logits, bias_row, _decay_selectors())


ATT_BLOCK = 256


def _transposed_queries(q):
    return q.astype(F32).T.astype(BF16)


def _head_halves(x, axis):
    low = lax.broadcasted_iota(jnp.int32, x.shape, axis) < HEAD_DIM
    zero = jnp.zeros_like(x)
    return jnp.where(low, x, zero), jnp.where(low, zero, x)


def _key_query_masks(t):
    key = lax.broadcasted_iota(jnp.int32, (t, t), 0)
    query = lax.broadcasted_iota(jnp.int32, (t, t), 1)
    return key <= query, key < query


def _bias_ones(first_lane):
    lane = lax.broadcasted_iota(jnp.int32, (ATT_BLOCK, LANES), 1)
    hit = (lane >= first_lane) & (lane < first_lane + BIAS_PIECES)
    return jnp.where(hit, 1.0, 0.0).astype(BF16)


def _store_values_t(v_ref, vt_refs):
    t = ATT_BLOCK
    for blk in range(v_ref.shape[0] // t):
        vt = v_ref[blk * t:(blk + 1) * t, :].astype(F32).T
        if len(vt_refs) == 2:
            va, vb = _head_halves(vt, 0)
            vt_refs[0][blk] = va.astype(BF16)
            vt_refs[1][blk] = vb.astype(BF16)
        else:
            vt_refs[0][blk] = vt.astype(BF16)


def _softmax_step_t(s, m, l):
    m_new = jnp.maximum(m, jnp.max(s, axis=0, keepdims=True))
    alpha = jnp.exp(m - m_new)
    p = jnp.exp(s - m_new)
    l = alpha * l + jnp.sum(p, axis=0, keepdims=True)
    return m_new, l, alpha, p.astype(BF16)


def _stat_init(t):
    return jnp.full((1, t), NEG_BIG, F32), jnp.zeros((1, t), F32)


def _fox_kernel(q_ref, k_ref, v_ref, kb_ref, o_ref, vta_ref, vtb_ref):
    t = ATT_BLOCK
    n_blocks = q_ref.shape[0] // t
    _store_values_t(v_ref, (vta_ref, vtb_ref))
    causal, _ = _key_query_masks(t)
    ones_a, ones_b = _bias_ones(0), _bias_ones(BIAS_PIECES)
    head_a_rows = lax.broadcasted_iota(jnp.int32, (LANES, t), 0) < HEAD_DIM

    def q_block(qi, _):
        q0 = pl.multiple_of(qi * t, t)
        qa, qb = _head_halves(q_ref[pl.ds(q0, t), :] * QK_SCALE, 1)
        qta = _transposed_queries(jnp.concatenate([qa, ones_a], axis=1))
        qtb = _transposed_queries(jnp.concatenate([qb, ones_b], axis=1))

        def scores(j):
            k0 = pl.multiple_of(j * t, t)
            k = jnp.concatenate([k_ref[pl.ds(k0, t), :], kb_ref[pl.ds(k0, t), :]], axis=1)
            return (jnp.dot(k, qta, preferred_element_type=F32),
                    jnp.dot(k, qtb, preferred_element_type=F32))

        def accumulate(acc, j, pa, pb, alpha_a, alpha_b):
            return (jnp.where(head_a_rows, alpha_a, alpha_b) * acc
                    + jnp.dot(vta_ref[j], pa, preferred_element_type=F32)
                    + jnp.dot(vtb_ref[j], pb, preferred_element_type=F32))

        def trip(j, c):
            sa, sb, pa, pb, alpha_a, alpha_b, ma, la, mb, lb, acc = c
            s_next = scores(j + 1)
            acc = accumulate(acc, jnp.maximum(j - 1, 0), pa, pb, alpha_a, alpha_b)
            ma, la, alpha_a, pa = _softmax_step_t(sa, ma, la)
            mb, lb, alpha_b, pb = _softmax_step_t(sb, mb, lb)
            return s_next + (pa, pb, alpha_a, alpha_b, ma, la, mb, lb, acc)

        no_p = jnp.zeros((t, t), BF16)
        one_row = jnp.ones((1, t), F32)
        init = (scores(0) + (no_p, no_p, one_row, one_row) + _stat_init(t) + _stat_init(t)
                + (jnp.zeros((LANES, t), F32),))
        sa, sb, pa, pb, alpha_a, alpha_b, ma, la, mb, lb, acc = lax.fori_loop(0, qi, trip, init)
        acc = accumulate(acc, jnp.maximum(qi - 1, 0), pa, pb, alpha_a, alpha_b)
        _, la, alpha_a, pa = _softmax_step_t(jnp.where(causal, sa, NEG_BIG), ma, la)
        _, lb, alpha_b, pb = _softmax_step_t(jnp.where(causal, sb, NEG_BIG), mb, lb)
        acc = accumulate(acc, qi, pa, pb, alpha_a, alpha_b)
        o_t = acc / jnp.where(head_a_rows, la, lb)
        o_ref[pl.ds(q0, t), :] = o_t.T.astype(o_ref.dtype)
        return 0

    lax.fori_loop(0, n_blocks, q_block, 0)


def _sb_kernel(q_ref, k_ref, v_ref, o_ref, vta_ref, vtb_ref):
    t = ATT_BLOCK
    n_blocks = q_ref.shape[0] // t
    _store_values_t(v_ref, (vta_ref, vtb_ref))
    _, strict = _key_query_masks(t)
    row = lax.broadcasted_iota(jnp.int32, (t, t), 0)
    col = lax.broadcasted_iota(jnp.int32, (t, t), 1)
    later_keys = jnp.where(col > row, 1.0, 0.0).astype(BF16)

    def q_block(qi, _):
        q0 = pl.multiple_of(qi * t, t)
        qa, qb = _head_halves(q_ref[pl.ds(q0, t), :] * QK_SCALE, 1)
        qta, qtb = _transposed_queries(qa), _transposed_queries(qb)

        def scores(step):
            j = jnp.maximum(qi - step, 0)
            k = k_ref[pl.ds(pl.multiple_of(j * t, t), t), :]
            return (jnp.dot(k, qta, preferred_element_type=F32),
                    jnp.dot(k, qtb, preferred_element_type=F32))

        def weights(z, later, diagonal):
            softplus = jnp.maximum(z, 0.0) + jnp.log(1.0 + jnp.exp(-jnp.abs(z)))
            log_stay = -softplus
            if diagonal:
                log_stay = jnp.where(strict, log_stay, 0.0)
            tail = jnp.dot(later_keys, log_stay.astype(BF16),
                           preferred_element_type=F32) + later
            a = jnp.exp(z + log_stay + tail)
            if diagonal:
                a = jnp.where(strict, a, 0.0)
            return a.astype(BF16), later + jnp.sum(log_stay, axis=0, keepdims=True)

        def accumulate(acc, step, aa, ab):
            j = qi - step
            return (acc + jnp.dot(vta_ref[j], aa, preferred_element_type=F32)
                    + jnp.dot(vtb_ref[j], ab, preferred_element_type=F32))

        def trip(step, c):
            za, zb, aa, ab, later_a, later_b, acc = c
            z_next = scores(step + 1)
            acc = accumulate(acc, step - 1, aa, ab)
            aa, later_a = weights(za, later_a, False)
            ab, later_b = weights(zb, later_b, False)
            return z_next + (aa, ab, later_a, later_b, acc)

        za, zb = scores(0)
        z_next = scores(1)
        zero_row = jnp.zeros((1, t), F32)
        aa, later_a = weights(za, zero_row, True)
        ab, later_b = weights(zb, zero_row, True)
        init = z_next + (aa, ab, later_a, later_b, jnp.zeros((LANES, t), F32))
        _, _, aa, ab, _, _, acc = lax.fori_loop(1, qi + 1, trip, init)
        acc = accumulate(acc, qi, aa, ab)
        o_ref[pl.ds(q0, t), :] = acc.T.astype(o_ref.dtype)
        return 0

    lax.fori_loop(0, n_blocks, q_block, 0)


def _diff_kernel(q_ref, k_ref, v_ref, slope_ref, dl_ref, g_ref, o_ref, vt_ref, kb_ref, *,
                 lam_init):
    t = ATT_BLOCK
    n_blocks = q_ref.shape[0] // t
    _store_values_t(v_ref, (vt_ref,))
    slope = slope_ref[:, 0:1]
    lane = lax.broadcasted_iota(jnp.int32, (t, LANES), 1)
    for blk in range(n_blocks):
        pos = (lax.broadcasted_iota(jnp.int32, (t, LANES), 0) + blk * t).astype(F32)
        p1, p2, p3 = _split3(slope * pos)
        bias = jnp.where(lane == 0, p1, jnp.where(lane == 1, p2, jnp.where(lane == 2, p3, 0.0)))
        kb_ref[blk * t:(blk + 1) * t, :] = bias.astype(BF16)
    causal, _ = _key_query_masks(t)
    ones = _bias_ones(0)
    dl = dl_ref[...]
    lam = (jnp.exp(jnp.sum(dl[0:1] * dl[1:2], axis=1, keepdims=True))
           - jnp.exp(jnp.sum(dl[2:3] * dl[3:4], axis=1, keepdims=True)) + lam_init)

    def q_block(qi, _):
        q0 = pl.multiple_of(qi * t, t)
        q1, q2 = _head_halves(q_ref[pl.ds(q0, t), :] * QK_SCALE, 1)
        qt1 = _transposed_queries(jnp.concatenate([q1, ones], axis=1))
        qt2 = _transposed_queries(jnp.concatenate([q2, ones], axis=1))

        def scores(j):
            k0 = pl.multiple_of(j * t, t)
            k = jnp.concatenate([k_ref[pl.ds(k0, t), :], kb_ref[pl.ds(k0, t), :]], axis=1)
            return (jnp.dot(k, qt1, preferred_element_type=F32),
                    jnp.dot(k, qt2, preferred_element_type=F32))

        def accumulate(acc1, acc2, j, p1, p2, alpha1, alpha2):
            vt = vt_ref[j]
            return (alpha1 * acc1 + jnp.dot(vt, p1, preferred_element_type=F32),
                    alpha2 * acc2 + jnp.dot(vt, p2, preferred_element_type=F32))

        def trip(j, c):
            s1, s2, p1, p2, alpha1, alpha2, m1, l1, m2, l2, acc1, acc2 = c
            s_next = scores(j + 1)
            acc1, acc2 = accumulate(acc1, acc2, jnp.maximum(j - 1, 0), p1, p2, alpha1, alpha2)
            m1, l1, alpha1, p1 = _softmax_step_t(s1, m1, l1)
            m2, l2, alpha2, p2 = _softmax_step_t(s2, m2, l2)
            return s_next + (p1, p2, alpha1, alpha2, m1, l1, m2, l2, acc1, acc2)

        no_p = jnp.zeros((t, t), BF16)
        one_row = jnp.ones((1, t), F32)
        zero_acc = jnp.zeros((LANES, t), F32)
        init = (scores(0) + (no_p, no_p, one_row, one_row) + _stat_init(t) + _stat_init(t)
                + (zero_acc, zero_acc))
        s1, s2, p1, p2, alpha1, alpha2, m1, l1, m2, l2, acc1, acc2 = lax.fori_loop(
            0, qi, trip, init)
        acc1, acc2 = accumulate(acc1, acc2, jnp.maximum(qi - 1, 0), p1, p2, alpha1, alpha2)
        _, l1, alpha1, p1 = _softmax_step_t(jnp.where(causal, s1, NEG_BIG), m1, l1)
        _, l2, alpha2, p2 = _softmax_step_t(jnp.where(causal, s2, NEG_BIG), m2, l2)
        acc1, acc2 = accumulate(acc1, acc2, qi, p1, p2, alpha1, alpha2)
        o = (acc1 / l1 - lam * (acc2 / l2)).T
        o_ref[pl.ds(q0, t), :] = (_rms(o, g_ref[...]) * (1.0 - lam_init)).astype(o_ref.dtype)
        return 0

    lax.fori_loop(0, n_blocks, q_block, 0)


def _attention_call(kernel, proj, colblks, extra_in, extra_specs, scratch, batch, seq,
                    n_colblk, name):
    qc, kc, vc = colblks
    return pl.pallas_call(
        kernel,
        grid=(batch, n_colblk),
        in_specs=[
            pl.BlockSpec((seq, LANES), lambda b, c: (b, qc + c)),
            pl.BlockSpec((seq, LANES), lambda b, c: (b, kc + c)),
            pl.BlockSpec((seq, LANES), lambda b, c: (b, vc + c)),
        ] + extra_specs,
        out_specs=pl.BlockSpec((seq, LANES), lambda b, c: (b, c)),
        out_shape=jax.ShapeDtypeStruct((batch * seq, n_colblk * LANES), BF16),
        scratch_shapes=scratch,
        compiler_params=_cparams(("parallel", "parallel")),
        name=name,
    )(proj, proj, proj, *extra_in)


def _values_t_scratch(seq):
    return pltpu.VMEM((seq // ATT_BLOCK, LANES, ATT_BLOCK), BF16)


ROUTE_EXPERT = 0
ROUTE_WEIGHT = 2


def _route(h, r_ref):
    h1 = h.astype(BF16)
    h2 = (h - h1.astype(F32)).astype(BF16)
    prod = (jnp.dot(h1, r_ref[...], preferred_element_type=F32)
            + jnp.dot(h2, r_ref[...], preferred_element_type=F32))
    lane = lax.broadcasted_iota(jnp.int32, prod.shape, 1)
    shifted1 = pltpu.roll(prod, LANES - N_EXPERTS, axis=1)
    shifted2 = pltpu.roll(prod, LANES - 2 * N_EXPERTS, axis=1)
    logits = jnp.where(lane < N_EXPERTS, prod + shifted1 + shifted2, NEG_BIG)
    v1 = jnp.max(logits, axis=1, keepdims=True)
    i1 = jnp.min(jnp.where(logits == v1, lane, LANES), axis=1, keepdims=True)
    rest = jnp.where(lane == i1, NEG_BIG, logits)
    v2 = jnp.max(rest, axis=1, keepdims=True)
    i2 = jnp.min(jnp.where(rest == v2, lane, LANES), axis=1, keepdims=True)
    e2 = jnp.exp(v2 - v1)
    w1 = 1.0 / (1.0 + e2)
    w2 = e2 / (1.0 + e2)
    record = jnp.where(lane == ROUTE_EXPERT, i1.astype(F32), 0.0)
    record = jnp.where(lane == ROUTE_EXPERT + 1, i2.astype(F32), record)
    record = jnp.where(lane == ROUTE_WEIGHT, w1, record)
    return jnp.where(lane == ROUTE_WEIGHT + 1, w2, record)


def _merge_kernel(*refs, with_router):
    if with_router:
        (of_ref, os_ref, od_ref, gl_ref, x_ref, wb_ref, wo_ref, g1_ref, gate_ref,
         g2_ref, sc_ref, sh_ref, r_ref, xo_ref, ho_ref, cw_ref) = refs
    else:
        (of_ref, os_ref, od_ref, gl_ref, x_ref, wb_ref, wo_ref, g1_ref, gate_ref,
         g2_ref, sc_ref, sh_ref, xo_ref, ho_ref) = refs
    merged = None
    for n, o_ref in enumerate((of_ref, os_ref, od_ref)):
        p = jnp.dot(o_ref[...], wb_ref[n], preferred_element_type=F32)
        gated = _sigmoid(gl_ref[:, n * D_MODEL:(n + 1) * D_MODEL].astype(F32)) * p
        merged = gated if merged is None else merged + gated
    y = jnp.dot(merged.astype(BF16), wo_ref[...], preferred_element_type=F32)
    xn = x_ref[...] + gate_ref[...] * _rms(y, g1_ref[...])
    xo_ref[...] = xn
    h = _rms(xn, g2_ref[...]) * (1.0 + sc_ref[...]) + sh_ref[...]
    ho_ref[...] = h.astype(ho_ref.dtype)
    if with_router:
        cw_ref[...] = _route(h, r_ref)


def _merge(o_fox, o_sb, o_diff, proj, x2d, w_branch, w_out, g1, g2, mod, mod_attn, mod_ffn,
           router_packed, seq):
    tokens = x2d.shape[0]
    tm = min(512, seq)
    tpb = seq // tm
    with_router = router_packed is not None
    row = lambda i: (i, 0)
    const2 = lambda i: (0, 0)
    in_specs = [
        pl.BlockSpec((tm, BRANCH_WIDTH), row),
        pl.BlockSpec((tm, BRANCH_WIDTH), row),
        pl.BlockSpec((tm, BRANCH_WIDTH), row),
        pl.BlockSpec((tm, GATE_COLS), row),
        pl.BlockSpec((tm, D_MODEL), row),
        pl.BlockSpec((N_BRANCHES, BRANCH_WIDTH, D_MODEL), lambda i: (0, 0, 0)),
        pl.BlockSpec((D_MODEL, D_MODEL), const2),
        pl.BlockSpec((1, D_MODEL), const2),
        _mod_spec(mod_attn, 2, tpb),
        pl.BlockSpec((1, D_MODEL), const2),
        _mod_spec(mod_ffn, 1, tpb),
        _mod_spec(mod_ffn, 0, tpb),
    ]
    args = [o_fox, o_sb, o_diff, proj, x2d, w_branch, w_out, g1, mod, g2, mod, mod]
    out_specs = [pl.BlockSpec((tm, D_MODEL), row), pl.BlockSpec((tm, D_MODEL), row)]
    out_shape = [jax.ShapeDtypeStruct((tokens, D_MODEL), F32),
                 jax.ShapeDtypeStruct((tokens, D_MODEL), F32 if with_router else BF16)]
    if with_router:
        in_specs.append(pl.BlockSpec((D_MODEL, LANES), const2))
        args.append(router_packed)
        out_specs.append(pl.BlockSpec((tm, LANES), row))
        out_shape.append(jax.ShapeDtypeStruct((tokens, LANES), F32))
    return pl.pallas_call(
        functools.partial(_merge_kernel, with_router=with_router),
        grid=(tokens // tm,),
        in_specs=in_specs,
        out_specs=out_specs,
        out_shape=out_shape,
        compiler_params=_cparams(("parallel",)),
        name="merge_router" if with_router else "merge",
    )(*args)


def _swiglu_partial(h, wg, wu):
    gate = jnp.dot(h, wg, preferred_element_type=F32)
    up = jnp.dot(h, wu, preferred_element_type=F32)
    return gate * _sigmoid(gate) * up


def _ffn_kernel(h_ref, wg_ref, wu_ref, wd_ref, x_ref, g_ref, gate_ref, o_ref, acc_ref):
    j = pl.program_id(1)

    @pl.when(j == 0)
    def _():
        acc_ref[...] = jnp.zeros_like(acc_ref)

    act = _swiglu_partial(h_ref[...], wg_ref[...], wu_ref[...])
    acc_ref[...] += jnp.dot(act.astype(BF16), wd_ref[...], preferred_element_type=F32)

    @pl.when(j == pl.num_programs(1) - 1)
    def _():
        o_ref[...] = x_ref[...] + gate_ref[...] * _rms(acc_ref[...], g_ref[...])


def _dense_ffn(h, x2d, wg, wu, wd, g, mod, mod_idx, seq):
    tokens = x2d.shape[0]
    tm = min(1024, seq)
    tf = 512
    tpb = seq // tm
    return pl.pallas_call(
        _ffn_kernel,
        grid=(tokens // tm, FFN_DIM // tf),
        in_specs=[
            pl.BlockSpec((tm, D_MODEL), lambda i, j: (i, 0)),
            pl.BlockSpec((D_MODEL, tf), lambda i, j: (0, j)),
            pl.BlockSpec((D_MODEL, tf), lambda i, j: (0, j)),
            pl.BlockSpec((tf, D_MODEL), lambda i, j: (j, 0)),
            pl.BlockSpec((tm, D_MODEL), lambda i, j: (i, 0)),
            pl.BlockSpec((1, D_MODEL), lambda i, j: (0, 0)),
            _mod_spec(mod_idx, 2, tpb),
        ],
        out_specs=pl.BlockSpec((tm, D_MODEL), lambda i, j: (i, 0)),
        out_shape=jax.ShapeDtypeStruct((tokens, D_MODEL), F32),
        scratch_shapes=[pltpu.VMEM((tm, D_MODEL), F32)],
        compiler_params=_cparams(("parallel", "arbitrary")),
        name="dense_ffn",
    )(h, wg, wu, wd, x2d, g, mod)


MOE_TILE = 512
RANK_BLOCK = 512


def _rank_kernel(route_ref, rank_ref, total_ref, count_scr):
    @pl.when(pl.program_id(0) == 0)
    def _():
        count_scr[...] = jnp.zeros_like(count_scr)

    route = route_ref[...]
    lane = lax.broadcasted_iota(jnp.int32, route.shape, 1)
    lane_f = lane.astype(F32)
    e1 = route[:, ROUTE_EXPERT:ROUTE_EXPERT + 1]
    e2 = route[:, ROUTE_EXPERT + 1:ROUTE_EXPERT + 2]
    member = jnp.where(lane_f == e1, 1.0, jnp.where(lane_f == e2, 1.0, 0.0))
    row = lax.broadcasted_iota(jnp.int32, (RANK_BLOCK, RANK_BLOCK), 0)
    col = lax.broadcasted_iota(jnp.int32, (RANK_BLOCK, RANK_BLOCK), 1)
    earlier = jnp.where(col < row, 1.0, 0.0).astype(BF16)
    before = jnp.dot(earlier, member.astype(BF16), preferred_element_type=F32) + count_scr[...]
    rank1 = jnp.sum(jnp.where(lane_f == e1, before, 0.0), axis=1, keepdims=True)
    rank2 = jnp.sum(jnp.where(lane_f == e2, before, 0.0), axis=1, keepdims=True)
    rank_ref[...] = jnp.where(lane == 0, rank1, jnp.where(lane == 1, rank2, 0.0))
    count_scr[...] += jnp.sum(member, axis=0, keepdims=True)
    total_ref[...] = count_scr[...]


def _expert_ranks(route):
    tokens = route.shape[0]
    return pl.pallas_call(
        _rank_kernel,
        grid=(tokens // RANK_BLOCK,),
        in_specs=[pl.BlockSpec((RANK_BLOCK, LANES), lambda i: (i, 0))],
        out_specs=[pl.BlockSpec((RANK_BLOCK, LANES), lambda i: (i, 0)),
                   pl.BlockSpec((1, LANES), lambda i: (0, 0))],
        out_shape=[jax.ShapeDtypeStruct((tokens, LANES), F32),
                   jax.ShapeDtypeStruct((1, LANES), F32)],
        scratch_shapes=[pltpu.VMEM((1, LANES), F32)],
        compiler_params=_cparams(("arbitrary",)),
        name="expert_ranks",
    )(route)


def _routing_tables(route, ranks, totals, n_tiles):
    counts = totals[0, :N_EXPERTS].astype(jnp.int32)
    padded = (counts + MOE_TILE - 1) // MOE_TILE * MOE_TILE
    ends = jnp.cumsum(padded)
    starts = ends - padded
    e1 = route[:, ROUTE_EXPERT].astype(jnp.int32)
    e2 = route[:, ROUTE_EXPERT + 1].astype(jnp.int32)
    pos1 = starts[e1] + ranks[:, 0].astype(jnp.int32)
    pos2 = starts[e2] + ranks[:, 1].astype(jnp.int32)
    n_used = ends[-1:] // MOE_TILE
    tile_start = jnp.minimum(jnp.arange(n_tiles, dtype=jnp.int32), n_used - 1) * MOE_TILE
    tile_expert = jnp.sum(tile_start[:, None] >= ends[None, :], axis=1).astype(jnp.int32)
    return pos1, pos2, tile_expert, n_used.astype(jnp.int32)


def _dispatch_kernel(pos1_ref, pos2_ref, h_hbm, init_hbm, xs_hbm, sem, *, tm):
    del init_hbm
    base = pl.program_id(0) * tm

    def row_copy(r, pos_ref):
        return pltpu.make_async_copy(
            h_hbm.at[pl.ds(base + r, 1)], xs_hbm.at[pl.ds(pos_ref[base + r], 1)], sem)

    def issue(r, _):
        row_copy(r, pos1_ref).start()
        row_copy(r, pos2_ref).start()
        return 0

    def drain(r, _):
        row_copy(r, pos1_ref).wait()
        row_copy(r, pos2_ref).wait()
        return 0

    lax.fori_loop(0, tm, issue, 0)
    lax.fori_loop(0, tm, drain, 0)


def _dispatch(h, pos1, pos2, n_rows):
    tokens = h.shape[0]
    tm = 512
    return pl.pallas_call(
        functools.partial(_dispatch_kernel, tm=tm),
        grid_spec=pltpu.PrefetchScalarGridSpec(
            num_scalar_prefetch=2,
            grid=(tokens // tm,),
            in_specs=[pl.BlockSpec(memory_space=pl.ANY), pl.BlockSpec(memory_space=pl.ANY)],
            out_specs=pl.BlockSpec(memory_space=pl.ANY),
            scratch_shapes=[pltpu.SemaphoreType.DMA],
        ),
        out_shape=jax.ShapeDtypeStruct((n_rows, D_MODEL), F32),
        input_output_aliases={3: 0},
        compiler_params=_cparams(("arbitrary",)),
        name="expert_dispatch",
    )(pos1, pos2, h, jnp.zeros((n_rows, D_MODEL), F32))


def _grouped_ffn_kernel(te_ref, nu_ref, xs_ref, wg_ref, wu_ref, wd_ref, ys_ref, xb_scr, acc_scr):
    del te_ref
    i = pl.program_id(0)
    j = pl.program_id(1)
    last = pl.num_programs(1) - 1
    used = i < nu_ref[0]

    @pl.when(used & (j == 0))
    def _():
        xb_scr[...] = xs_ref[...].astype(BF16)
        acc_scr[...] = jnp.zeros_like(acc_scr)

    @pl.when(used)
    def _():
        act = _swiglu_partial(xb_scr[...], wg_ref[...], wu_ref[...])
        acc_scr[...] += jnp.dot(act.astype(BF16), wd_ref[...], preferred_element_type=F32)

    @pl.when(used & (j == last))
    def _():
        ys_ref[...] = acc_scr[...]

    @pl.when(jnp.logical_not(used) & (j == last))
    def _():
        ys_ref[...] = jnp.zeros_like(ys_ref)


def _grouped_ffn(xs, tile_expert, n_used, wg, wu, wd):
    n_rows = xs.shape[0]
    tf = 512
    nj = FFN_DIM // tf

    def ffn_col(i, j, nu):
        return jnp.where(i < nu[0], j, nj - 1)

    return pl.pallas_call(
        _grouped_ffn_kernel,
        grid_spec=pltpu.PrefetchScalarGridSpec(
            num_scalar_prefetch=2,
            grid=(n_rows // MOE_TILE, nj),
            in_specs=[
                pl.BlockSpec((MOE_TILE, D_MODEL),
                             lambda i, j, te, nu: (jnp.minimum(i, nu[0] - 1), 0)),
                pl.BlockSpec((None, D_MODEL, tf),
                             lambda i, j, te, nu: (te[i], 0, ffn_col(i, j, nu))),
                pl.BlockSpec((None, D_MODEL, tf),
                             lambda i, j, te, nu: (te[i], 0, ffn_col(i, j, nu))),
                pl.BlockSpec((None, tf, D_MODEL),
                             lambda i, j, te, nu: (te[i], ffn_col(i, j, nu), 0)),
            ],
            out_specs=pl.BlockSpec((MOE_TILE, D_MODEL), lambda i, j, te, nu: (i, 0)),
            scratch_shapes=[pltpu.VMEM((MOE_TILE, D_MODEL), BF16),
                            pltpu.VMEM((MOE_TILE, D_MODEL), F32)],
        ),
        out_shape=jax.ShapeDtypeStruct((n_rows, D_MODEL), F32),
        compiler_params=_cparams(("arbitrary", "arbitrary")),
        name="grouped_ffn",
    )(tile_expert, n_used, xs, wg, wu, wd)


def _combine_kernel(pos1_ref, pos2_ref, ys_hbm, route_ref, x_ref, g_ref, gate_ref, o_ref,
                    y1_scr, y2_scr, sem, *, tm):
    base = pl.program_id(0) * tm

    def row_copy(r, pos_ref, dst):
        return pltpu.make_async_copy(
            ys_hbm.at[pl.ds(pos_ref[base + r], 1)], dst.at[pl.ds(r, 1)], sem)

    def issue(r, _):
        row_copy(r, pos1_ref, y1_scr).start()
        row_copy(r, pos2_ref, y2_scr).start()
        return 0

    def drain(r, _):
        row_copy(r, pos1_ref, y1_scr).wait()
        row_copy(r, pos2_ref, y2_scr).wait()
        return 0

    lax.fori_loop(0, tm, issue, 0)
    lax.fori_loop(0, tm, drain, 0)
    route = route_ref[...]
    w1 = route[:, ROUTE_WEIGHT:ROUTE_WEIGHT + 1]
    w2 = route[:, ROUTE_WEIGHT + 1:ROUTE_WEIGHT + 2]
    y = w1 * y1_scr[...] + w2 * y2_scr[...]
    o_ref[...] = x_ref[...] + gate_ref[...] * _rms(y, g_ref[...])


def _combine(ys, pos1, pos2, route, x2d, g, mod, mod_idx, seq):
    tokens = x2d.shape[0]
    tm = min(256, seq)
    tpb = seq // tm
    gate_spec = _mod_spec(mod_idx, 2, tpb)
    return pl.pallas_call(
        functools.partial(_combine_kernel, tm=tm),
        grid_spec=pltpu.PrefetchScalarGridSpec(
            num_scalar_prefetch=2,
            grid=(tokens // tm,),
            in_specs=[
                pl.BlockSpec(memory_space=pl.ANY),
                pl.BlockSpec((tm, LANES), lambda i, p1, p2: (i, 0)),
                pl.BlockSpec((tm, D_MODEL), lambda i, p1, p2: (i, 0)),
                pl.BlockSpec((1, D_MODEL), lambda i, p1, p2: (0, 0)),
                pl.BlockSpec(gate_spec.block_shape, lambda i, p1, p2: gate_spec.index_map(i)),
            ],
            out_specs=pl.BlockSpec((tm, D_MODEL), lambda i, p1, p2: (i, 0)),
            scratch_shapes=[pltpu.VMEM((tm, D_MODEL), F32), pltpu.VMEM((tm, D_MODEL), F32),
                            pltpu.SemaphoreType.DMA],
        ),
        out_shape=jax.ShapeDtypeStruct((tokens, D_MODEL), F32),
        compiler_params=_cparams(("arbitrary",)),
        name="expert_combine",
    )(pos1, pos2, ys, route, x2d, g, mod)


def _moe_ffn(h, route, x2d, wg, wu, wd, g, mod, mod_idx, seq):
    tokens = x2d.shape[0]
    n_rows = 2 * tokens + N_EXPERTS * MOE_TILE
    ranks, totals = _expert_ranks(route)
    pos1, pos2, tile_expert, n_used = _routing_tables(route, ranks, totals, n_rows // MOE_TILE)
    xs = _dispatch(h, pos1, pos2, n_rows)
    ys = _grouped_ffn(xs, tile_expert, n_used, wg, wu, wd)
    return _combine(ys, pos1, pos2, route, x2d, g, mod, mod_idx, seq)


def _prep_w_in(w):
    fw = FOX_HEADS * HEAD_DIM
    sw = SB_HEADS * HEAD_DIM
    dw = DIFF_HEADS * 2 * HEAD_DIM
    sizes = (fw, fw, fw, FOX_HEADS, sw, sw, sw, dw, dw, dw, GATE_COLS)
    offs = [0]
    for s in sizes:
        offs.append(offs[-1] + s)
    part = [w[:, offs[i]:offs[i + 1]] for i in range(len(sizes))]
    fq, fk, fv, ff, sq, sk, sv, dq, dk, dv, gl = part
    w_main = jnp.concatenate([gl, fq, fk, fv, sq, sk, sv, dq, dk, dv], axis=1).astype(BF16)
    w_forget = jnp.pad(ff, ((0, 0), (0, LANES - FOX_HEADS))).astype(BF16)
    return w_main, w_forget


def _pack_router(r):
    p1 = r.astype(BF16)
    r1 = r - p1.astype(F32)
    p2 = r1.astype(BF16)
    p3 = (r1 - p2.astype(F32)).astype(BF16)
    packed = jnp.concatenate([p1, p2, p3], axis=1)
    return jnp.pad(packed, ((0, 0), (0, LANES - 3 * N_EXPERTS)))


def kernel(x, c, w_ada, b_ada, norm_g, w_in, fox_f_bias, diff_lambda, diff_subln_g, w_branch,
           w_out, ffn_w_gate, ffn_w_up, ffn_w_down, moe_router, moe_w_gate, moe_w_up,
           moe_w_down):
    batch, seq, _ = x.shape
    depth = w_in.shape[0]
    tokens = batch * seq
    x2d = x.reshape(tokens, D_MODEL)
    mod = _ada_modulation(c, w_ada, b_ada)
    slopes = jnp.exp2(-ALIBI_MAX_BIAS * jnp.arange(1, DIFF_HEADS + 1, dtype=F32) / DIFF_HEADS)
    slopes = jnp.broadcast_to(slopes[:, None, None], (DIFF_HEADS, 1, LANES))

    for layer in range(depth):
        lam_init = 0.8 - 0.6 * math.exp(-0.3 * layer)
        mod_attn, mod_ffn = 2 * layer, 2 * layer + 1
        w_main, w_forget = _prep_w_in(w_in[layer])
        proj, forget_logits = _in_projection(
            x2d, norm_g[layer, 0][None, :], mod, mod_attn, w_main, w_forget, seq)
        key_bias = _fox_decay(forget_logits, fox_f_bias[layer], batch, seq)

        o_fox = _attention_call(
            _fox_kernel, proj, (COLBLK_FQ, COLBLK_FK, COLBLK_FV), [key_bias],
            [pl.BlockSpec((seq, LANES), lambda b, c: (b, c))],
            [_values_t_scratch(seq), _values_t_scratch(seq)],
            batch, seq, FOX_HEADS // 2, "fox_attention")
        o_sb = _attention_call(
            _sb_kernel, proj, (COLBLK_SQ, COLBLK_SK, COLBLK_SV), [], [],
            [_values_t_scratch(seq), _values_t_scratch(seq)],
            batch, seq, SB_HEADS // 2, "stickbreak_attention")
        o_diff = _attention_call(
            functools.partial(_diff_kernel, lam_init=lam_init), proj,
            (COLBLK_DQ, COLBLK_DK, COLBLK_DV),
            [slopes, diff_lambda[layer].astype(F32), diff_subln_g[layer][None, :].astype(F32)],
            [pl.BlockSpec((None, 1, LANES), lambda b, c: (c, 0, 0)),
             pl.BlockSpec((4, HEAD_DIM), lambda b, c: (0, 0)),
             pl.BlockSpec((1, 2 * HEAD_DIM), lambda b, c: (0, 0))],
            [_values_t_scratch(seq), pltpu.VMEM((seq, LANES), BF16)],
            batch, seq, DIFF_HEADS, "diff_attention")

        is_moe = layer % 2 == 1
        idx = layer // 2
        router_packed = _pack_router(moe_router[idx]) if is_moe else None
        merged = _merge(
            o_fox, o_sb, o_diff, proj, x2d, w_branch[layer].astype(BF16),
            w_out[layer].astype(BF16), norm_g[layer, 1][None, :], norm_g[layer, 2][None, :],
            mod, mod_attn, mod_ffn, router_packed, seq)
        g3 = norm_g[layer, 3][None, :]
        if is_moe:
            x2d, h, combine = merged
            x2d = _moe_ffn(h, combine, x2d, moe_w_gate[idx].astype(BF16),
                           moe_w_up[idx].astype(BF16), moe_w_down[idx].astype(BF16),
                           g3, mod, mod_ffn, seq)
        else:
            x2d, h = merged
            x2d = _dense_ffn(h, x2d, ffn_w_gate[idx].astype(BF16), ffn_w_up[idx].astype(BF16),
                             ffn_w_down[idx].astype(BF16), g3, mod, mod_ffn, seq)
    return x2d.reshape(batch, seq, D_MODEL)
```

```python
import functools
import math

import jax
import jax.numpy as jnp
import numpy as np
from jax import lax
from jax.experimental import pallas as pl
from jax.experimental.pallas import tpu as pltpu

F32 = jnp.float32
BF16 = jnp.bfloat16

D_MODEL = 1024
HEAD_DIM = 64
FOX_HEADS = 8
SB_HEADS = 8
DIFF_HEADS = 4
BRANCH_WIDTH = 512
N_BRANCHES = 3
FFN_DIM = 3584
N_EXPERTS = 8
RMS_EPS = 1e-6
ALIBI_MAX_BIAS = 8.0

LANES = 128
GATE_COLS = N_BRANCHES * D_MODEL
COLBLK_FQ = GATE_COLS // LANES
COLBLK_FK = COLBLK_FQ + 4
COLBLK_FV = COLBLK_FK + 4
COLBLK_SQ = COLBLK_FV + 4
COLBLK_SK = COLBLK_SQ + 4
COLBLK_SV = COLBLK_SK + 4
COLBLK_DQ = COLBLK_SV + 4
COLBLK_DK = COLBLK_DQ + 4
COLBLK_DV = COLBLK_DK + 4
PROJ_COLS = (COLBLK_DV + 4) * LANES

VMEM_LIMIT = 56 * 1024 * 1024
NEG_BIG = -1e30
QK_SCALE = HEAD_DIM ** -0.5


def _cparams(sem):
    return pltpu.CompilerParams(dimension_semantics=sem, vmem_limit_bytes=VMEM_LIMIT)


def _sigmoid(v):
    return 1.0 / (1.0 + jnp.exp(-v))


def _rms(v, g):
    ms = jnp.mean(v * v, axis=-1, keepdims=True)
    return v * lax.rsqrt(ms + RMS_EPS) * g


def _ada_kernel(c_ref, w_ref, b_ref, o_ref):
    c = c_ref[...]
    a = (c * _sigmoid(c)).astype(BF16)
    o_ref[...] = jnp.dot(a, w_ref[...].astype(BF16), preferred_element_type=F32) + b_ref[...]


def _ada_modulation(c, w_ada, b_ada):
    n_mod = w_ada.shape[0] * w_ada.shape[1]
    batch = c.shape[0]
    w = w_ada.reshape(n_mod, D_MODEL, 3 * D_MODEL)
    b = b_ada.reshape(n_mod, 1, 3 * D_MODEL)
    out = pl.pallas_call(
        _ada_kernel,
        grid=(n_mod, 3),
        in_specs=[
            pl.BlockSpec((batch, D_MODEL), lambda m, j: (0, 0)),
            pl.BlockSpec((None, D_MODEL, D_MODEL), lambda m, j: (m, 0, j)),
            pl.BlockSpec((None, 1, D_MODEL), lambda m, j: (m, 0, j)),
        ],
        out_specs=pl.BlockSpec((None, batch, D_MODEL), lambda m, j: (m, 0, j)),
        out_shape=jax.ShapeDtypeStruct((n_mod, batch, 3 * D_MODEL), F32),
        compiler_params=_cparams(("parallel", "parallel")),
        name="ada_modulation",
    )(c, w, b)
    return out.reshape(n_mod, batch, 1, 3 * D_MODEL)


def _mod_spec(mod_idx, part, rows_per_batch_tiles):
    def index(i, *_):
        return (mod_idx, i // rows_per_batch_tiles, 0, part)
    return pl.BlockSpec((None, None, 1, D_MODEL), index)


def _inproj_kernel(x_ref, g_ref, sc_ref, sh_ref, w_ref, wf_ref, o_ref, f_ref, h_scr):
    @pl.when(pl.program_id(1) == 0)
    def _():
        h = _rms(x_ref[...], g_ref[...]) * (1.0 + sc_ref[...]) + sh_ref[...]
        hb = h.astype(BF16)
        h_scr[...] = hb
        f_ref[...] = jnp.dot(hb, wf_ref[...], preferred_element_type=F32)

    o_ref[...] = jnp.dot(h_scr[...], w_ref[...], preferred_element_type=F32).astype(BF16)


def _in_projection(x2d, g, mod, mod_idx, w_main, w_forget, seq):
    tokens = x2d.shape[0]
    tm = min(1024, seq)
    tn = 1536
    tiles_per_batch = seq // tm
    return pl.pallas_call(
        _inproj_kernel,
        grid=(tokens // tm, PROJ_COLS // tn),
        in_specs=[
            pl.BlockSpec((tm, D_MODEL), lambda i, j: (i, 0)),
            pl.BlockSpec((1, D_MODEL), lambda i, j: (0, 0)),
            _mod_spec(mod_idx, 1, tiles_per_batch),
            _mod_spec(mod_idx, 0, tiles_per_batch),
            pl.BlockSpec((D_MODEL, tn), lambda i, j: (0, j)),
            pl.BlockSpec((D_MODEL, LANES), lambda i, j: (0, 0)),
        ],
        out_specs=[
            pl.BlockSpec((tm, tn), lambda i, j: (i, j)),
            pl.BlockSpec((tm, LANES), lambda i, j: (i, 0)),
        ],
        out_shape=[
            jax.ShapeDtypeStruct((tokens, PROJ_COLS), BF16),
            jax.ShapeDtypeStruct((tokens, LANES), F32),
        ],
        scratch_shapes=[pltpu.VMEM((tm, D_MODEL), BF16)],
        compiler_params=_cparams(("parallel", "arbitrary")),
        name="in_projection",
    )(x2d, g, mod, mod, w_main, w_forget)


CUMSUM_BLOCK = 256
BIAS_PIECES = 3


def _split3(v):
    p1 = v.astype(BF16).astype(F32)
    r1 = v - p1
    p2 = r1.astype(BF16).astype(F32)
    p3 = (r1 - p2).astype(BF16).astype(F32)
    return p1, p2, p3


def _decay_kernel(f_ref, b_ref, sel_ref, o_ref):
    seq = f_ref.shape[0]
    row = lax.broadcasted_iota(jnp.int32, (CUMSUM_BLOCK, CUMSUM_BLOCK), 0)
    col = lax.broadcasted_iota(jnp.int32, (CUMSUM_BLOCK, CUMSUM_BLOCK), 1)
    tri = jnp.where(col <= row, 1.0, 0.0).astype(BF16)
    carry = jnp.zeros((1, LANES), F32)
    for blk in range(seq // CUMSUM_BLOCK):
        rows = slice(blk * CUMSUM_BLOCK, (blk + 1) * CUMSUM_BLOCK)
        z = f_ref[rows, :] + b_ref[...]
        logf = jnp.minimum(z, 0.0) - jnp.log1p(jnp.exp(-jnp.abs(z)))
        cum = carry
        for piece in _split3(logf):
            cum = cum + jnp.dot(tri, piece.astype(BF16), preferred_element_type=F32)
        carry = cum[CUMSUM_BLOCK - 1:CUMSUM_BLOCK, :]
        bias = None
        for i, piece in enumerate(_split3(-cum)):
            part = jnp.dot(piece.astype(BF16), sel_ref[i], preferred_element_type=F32)
            bias = part if bias is None else bias + part
        o_ref[rows, :] = bias.astype(BF16)


def _decay_selectors():
    sel = np.zeros((BIAS_PIECES, LANES, FOX_HEADS // 2 * LANES), np.float32)
    for h in range(FOX_HEADS):
        for i in range(BIAS_PIECES):
            sel[i, h, (h // 2) * LANES + (h % 2) * BIAS_PIECES + i] = 1.0
    return jnp.asarray(sel, BF16)


def _fox_decay(forget_logits, bias, batch, seq):
    bias_row = jnp.zeros((1, LANES), F32).at[0, :FOX_HEADS].set(bias.astype(F32))
    width = FOX_HEADS // 2 * LANES
    return pl.pallas_call(
        _decay_kernel,
        grid=(batch,),
        in_specs=[
            pl.BlockSpec((seq, LANES), lambda b: (b, 0)),
            pl.BlockSpec((1, LANES), lambda b: (0, 0)),
            pl.BlockSpec((BIAS_PIECES, LANES, width), lambda b: (0, 0, 0)),
        ],
        out_specs=pl.BlockSpec((seq, width), lambda b: (b, 0)),
        out_shape=jax.ShapeDtypeStruct((batch * seq, width), BF16),
        compiler_params=_cparams(("parallel",)),
        name="fox_decay",
    )(forget_logits, bias_row, _decay_selectors())


ATT_BLOCK = 256


def _transposed_queries(q):
    return q.astype(F32).T.astype(BF16)


def _head_halves(x, axis):
    low = lax.broadcasted_iota(jnp.int32, x.shape, axis) < HEAD_DIM
    zero = jnp.zeros_like(x)
    return jnp.where(low, x, zero), jnp.where(low, zero, x)


def _key_query_masks(t):
    key = lax.broadcasted_iota(jnp.int32, (t, t), 0)
    query = lax.broadcasted_iota(jnp.int32, (t, t), 1)
    return key <= query, key < query


def _bias_ones(first_lane):
    lane = lax.broadcasted_iota(jnp.int32, (ATT_BLOCK, LANES), 1)
    hit = (lane >= first_lane) & (lane < first_lane + BIAS_PIECES)
    return jnp.where(hit, 1.0, 0.0).astype(BF16)


def _store_values_t(v_ref, vt_refs):
    t = ATT_BLOCK
    for blk in range(v_ref.shape[0] // t):
        vt = v_ref[blk * t:(blk + 1) * t, :].astype(F32).T
        if len(vt_refs) == 2:
            va, vb = _head_halves(vt, 0)
            vt_refs[0][blk] = va.astype(BF16)
            vt_refs[1][blk] = vb.astype(BF16)
        else:
            vt_refs[0][blk] = vt.astype(BF16)


def _softmax_step_t(s, m, l):
    m_new = jnp.maximum(m, jnp.max(s, axis=0, keepdims=True))
    alpha = jnp.exp(m - m_new)
    p = jnp.exp(s - m_new)
    l = alpha * l + jnp.sum(p, axis=0, keepdims=True)
    return m_new, l, alpha, p.astype(BF16)


def _stat_init(t):
    return jnp.full((1, t), NEG_BIG, F32), jnp.zeros((1, t), F32)


def _fox_kernel(q_ref, k_ref, v_ref, kb_ref, o_ref, vta_ref, vtb_ref):
    t = ATT_BLOCK
    n_blocks = q_ref.shape[0] // t
    _store_values_t(v_ref, (vta_ref, vtb_ref))
    causal, _ = _key_query_masks(t)
    ones_a, ones_b = _bias_ones(0), _bias_ones(BIAS_PIECES)
    head_a_rows = lax.broadcasted_iota(jnp.int32, (LANES, t), 0) < HEAD_DIM

    def q_block(qi, _):
        q0 = pl.multiple_of(qi * t, t)
        qa, qb = _head_halves(q_ref[pl.ds(q0, t), :] * QK_SCALE, 1)
        qta = _transposed_queries(jnp.concatenate([qa, ones_a], axis=1))
        qtb = _transposed_queries(jnp.concatenate([qb, ones_b], axis=1))

        def scores(j):
            k0 = pl.multiple_of(j * t, t)
            k = jnp.concatenate([k_ref[pl.ds(k0, t), :], kb_ref[pl.ds(k0, t), :]], axis=1)
            return (jnp.dot(k, qta, preferred_element_type=F32),
                    jnp.dot(k, qtb, preferred_element_type=F32))

        def accumulate(acc, j, pa, pb, alpha_a, alpha_b):
            return (jnp.where(head_a_rows, alpha_a, alpha_b) * acc
                    + jnp.dot(vta_ref[j], pa, preferred_element_type=F32)
                    + jnp.dot(vtb_ref[j], pb, preferred_element_type=F32))

        def trip(j, c):
            sa, sb, pa, pb, alpha_a, alpha_b, ma, la, mb, lb, acc = c
            s_next = scores(j + 1)
            acc = accumulate(acc, jnp.maximum(j - 1, 0), pa, pb, alpha_a, alpha_b)
            ma, la, alpha_a, pa = _softmax_step_t(sa, ma, la)
            mb, lb, alpha_b, pb = _softmax_step_t(sb, mb, lb)
            return s_next + (pa, pb, alpha_a, alpha_b, ma, la, mb, lb, acc)

        no_p = jnp.zeros((t, t), BF16)
        one_row = jnp.ones((1, t), F32)
        init = (scores(0) + (no_p, no_p, one_row, one_row) + _stat_init(t) + _stat_init(t)
                + (jnp.zeros((LANES, t), F32),))
        sa, sb, pa, pb, alpha_a, alpha_b, ma, la, mb, lb, acc = lax.fori_loop(0, qi, trip, init)
        acc = accumulate(acc, jnp.maximum(qi - 1, 0), pa, pb, alpha_a, alpha_b)
        _, la, alpha_a, pa = _softmax_step_t(jnp.where(causal, sa, NEG_BIG), ma, la)
        _, lb, alpha_b, pb = _softmax_step_t(jnp.where(causal, sb, NEG_BIG), mb, lb)
        acc = accumulate(acc, qi, pa, pb, alpha_a, alpha_b)
        o_t = acc / jnp.where(head_a_rows, la, lb)
        o_ref[pl.ds(q0, t), :] = o_t.T.astype(o_ref.dtype)
        return 0

    lax.fori_loop(0, n_blocks, q_block, 0)


def _sb_kernel(q_ref, k_ref, v_ref, o_ref, vta_ref, vtb_ref):
    t = ATT_BLOCK
    n_blocks = q_ref.shape[0] // t
    _store_values_t(v_ref, (vta_ref, vtb_ref))
    _, strict = _key_query_masks(t)
    row = lax.broadcasted_iota(jnp.int32, (t, t), 0)
    col = lax.broadcasted_iota(jnp.int32, (t, t), 1)
    later_keys = jnp.where(col > row, 1.0, 0.0).astype(BF16)

    def q_block(qi, _):
        q0 = pl.multiple_of(qi * t, t)
        qa, qb = _head_halves(q_ref[pl.ds(q0, t), :] * QK_SCALE, 1)
        qta, qtb = _transposed_queries(qa), _transposed_queries(qb)

        def scores(step):
            j = jnp.maximum(qi - step, 0)
            k = k_ref[pl.ds(pl.multiple_of(j * t, t), t), :]
            return (jnp.dot(k, qta, preferred_element_type=F32),
                    jnp.dot(k, qtb, preferred_element_type=F32))

        def weights(z, later, diagonal):
            softplus = jnp.maximum(z, 0.0) + jnp.log(1.0 + jnp.exp(-jnp.abs(z)))
            log_stay = -softplus
            if diagonal:
                log_stay = jnp.where(strict, log_stay, 0.0)
            tail = jnp.dot(later_keys, log_stay.astype(BF16),
                           preferred_element_type=F32) + later
            a = jnp.exp(z + log_stay + tail)
            if diagonal:
                a = jnp.where(strict, a, 0.0)
            return a.astype(BF16), later + jnp.sum(log_stay, axis=0, keepdims=True)

        def accumulate(acc, step, aa, ab):
            j = qi - step
            return (acc + jnp.dot(vta_ref[j], aa, preferred_element_type=F32)
                    + jnp.dot(vtb_ref[j], ab, preferred_element_type=F32))

        def trip(step, c):
            za, zb, aa, ab, later_a, later_b, acc = c
            z_next = scores(step + 1)
            acc = accumulate(acc, step - 1, aa, ab)
            aa, later_a = weights(za, later_a, False)
            ab, later_b = weights(zb, later_b, False)
            return z_next + (aa, ab, later_a, later_b, acc)

        za, zb = scores(0)
        z_next = scores(1)
        zero_row = jnp.zeros((1, t), F32)
        aa, later_a = weights(za, zero_row, True)
        ab, later_b = weights(zb, zero_row, True)
        init = z_next + (aa, ab, later_a, later_b, jnp.zeros((LANES, t), F32))
        _, _, aa, ab, _, _, acc = lax.fori_loop(1, qi + 1, trip, init)
        acc = accumulate(acc, qi, aa, ab)
        o_ref[pl.ds(q0, t), :] = acc.T.astype(o_ref.dtype)
        return 0

    lax.fori_loop(0, n_blocks, q_block, 0)


def _diff_kernel(q_ref, k_ref, v_ref, slope_ref, dl_ref, g_ref, o_ref, vt_ref, kb_ref, *,
                 lam_init):
    t = ATT_BLOCK
    n_blocks = q_ref.shape[0] // t
    _store_values_t(v_ref, (vt_ref,))
    slope = slope_ref[:, 0:1]
    lane = lax.broadcasted_iota(jnp.int32, (t, LANES), 1)
    for blk in range(n_blocks):
        pos = (lax.broadcasted_iota(jnp.int32, (t, LANES), 0) + blk * t).astype(F32)
        p1, p2, p3 = _split3(slope * pos)
        bias = jnp.where(lane == 0, p1, jnp.where(lane == 1, p2, jnp.where(lane == 2, p3, 0.0)))
        kb_ref[blk * t:(blk + 1) * t, :] = bias.astype(BF16)
    causal, _ = _key_query_masks(t)
    ones = _bias_ones(0)
    dl = dl_ref[...]
    lam = (jnp.exp(jnp.sum(dl[0:1] * dl[1:2], axis=1, keepdims=True))
           - jnp.exp(jnp.sum(dl[2:3] * dl[3:4], axis=1, keepdims=True)) + lam_init)

    def q_block(qi, _):
        q0 = pl.multiple_of(qi * t, t)
        q1, q2 = _head_halves(q_ref[pl.ds(q0, t), :] * QK_SCALE, 1)
        qt1 = _transposed_queries(jnp.concatenate([q1, ones], axis=1))
        qt2 = _transposed_queries(jnp.concatenate([q2, ones], axis=1))

        def scores(j):
            k0 = pl.multiple_of(j * t, t)
            k = jnp.concatenate([k_ref[pl.ds(k0, t), :], kb_ref[pl.ds(k0, t), :]], axis=1)
            return (jnp.dot(k, qt1, preferred_element_type=F32),
                    jnp.dot(k, qt2, preferred_element_type=F32))

        def accumulate(acc1, acc2, j, p1, p2, alpha1, alpha2):
            vt = vt_ref[j]
            return (alpha1 * acc1 + jnp.dot(vt, p1, preferred_element_type=F32),
                    alpha2 * acc2 + jnp.dot(vt, p2, preferred_element_type=F32))

        def trip(j, c):
            s1, s2, p1, p2, alpha1, alpha2, m1, l1, m2, l2, acc1, acc2 = c
            s_next = scores(j + 1)
            acc1, acc2 = accumulate(acc1, acc2, jnp.maximum(j - 1, 0), p1, p2, alpha1, alpha2)
            m1, l1, alpha1, p1 = _softmax_step_t(s1, m1, l1)
            m2, l2, alpha2, p2 = _softmax_step_t(s2, m2, l2)
            return s_next + (p1, p2, alpha1, alpha2, m1, l1, m2, l2, acc1, acc2)

        no_p = jnp.zeros((t, t), BF16)
        one_row = jnp.ones((1, t), F32)
        zero_acc = jnp.zeros((LANES, t), F32)
        init = (scores(0) + (no_p, no_p, one_row, one_row) + _stat_init(t) + _stat_init(t)
                + (zero_acc, zero_acc))
        s1, s2, p1, p2, alpha1, alpha2, m1, l1, m2, l2, acc1, acc2 = lax.fori_loop(
            0, qi, trip, init)
        acc1, acc2 = accumulate(acc1, acc2, jnp.maximum(qi - 1, 0), p1, p2, alpha1, alpha2)
        _, l1, alpha1, p1 = _softmax_step_t(jnp.where(causal, s1, NEG_BIG), m1, l1)
        _, l2, alpha2, p2 = _softmax_step_t(jnp.where(causal, s2, NEG_BIG), m2, l2)
        acc1, acc2 = accumulate(acc1, acc2, qi, p1, p2, alpha1, alpha2)
        o = (acc1 / l1 - lam * (acc2 / l2)).T
        o_ref[pl.ds(q0, t), :] = (_rms(o, g_ref[...]) * (1.0 - lam_init)).astype(o_ref.dtype)
        return 0

    lax.fori_loop(0, n_blocks, q_block, 0)


def _attention_call(kernel, proj, colblks, extra_in, extra_specs, scratch, batch, seq,
                    n_colblk, name):
    qc, kc, vc = colblks
    return pl.pallas_call(
        kernel,
        grid=(batch, n_colblk),
        in_specs=[
            pl.BlockSpec((seq, LANES), lambda b, c: (b, qc + c)),
            pl.BlockSpec((seq, LANES), lambda b, c: (b, kc + c)),
            pl.BlockSpec((seq, LANES), lambda b, c: (b, vc + c)),
        ] + extra_specs,
        out_specs=pl.BlockSpec((seq, LANES), lambda b, c: (b, c)),
        out_shape=jax.ShapeDtypeStruct((batch * seq, n_colblk * LANES), BF16),
        scratch_shapes=scratch,
        compiler_params=_cparams(("parallel", "parallel")),
        name=name,
    )(proj, proj, proj, *extra_in)


def _values_t_scratch(seq):
    return pltpu.VMEM((seq // ATT_BLOCK, LANES, ATT_BLOCK), BF16)


ROUTE_EXPERT = 0
ROUTE_WEIGHT = 2


def _route(h, r_ref):
    h1 = h.astype(BF16)
    h2 = (h - h1.astype(F32)).astype(BF16)
    prod = (jnp.dot(h1, r_ref[...], preferred_element_type=F32)
            + jnp.dot(h2, r_ref[...], preferred_element_type=F32))
    lane = lax.broadcasted_iota(jnp.int32, prod.shape, 1)
    shifted1 = pltpu.roll(prod, LANES - N_EXPERTS, axis=1)
    shifted2 = pltpu.roll(prod, LANES - 2 * N_EXPERTS, axis=1)
    logits = jnp.where(lane < N_EXPERTS, prod + shifted1 + shifted2, NEG_BIG)
    v1 = jnp.max(logits, axis=1, keepdims=True)
    i1 = jnp.min(jnp.where(logits == v1, lane, LANES), axis=1, keepdims=True)
    rest = jnp.where(lane == i1, NEG_BIG, logits)
    v2 = jnp.max(rest, axis=1, keepdims=True)
    i2 = jnp.min(jnp.where(rest == v2, lane, LANES), axis=1, keepdims=True)
    e2 = jnp.exp(v2 - v1)
    w1 = 1.0 / (1.0 + e2)
    w2 = e2 / (1.0 + e2)
    record = jnp.where(lane == ROUTE_EXPERT, i1.astype(F32), 0.0)
    record = jnp.where(lane == ROUTE_EXPERT + 1, i2.astype(F32), record)
    record = jnp.where(lane == ROUTE_WEIGHT, w1, record)
    return jnp.where(lane == ROUTE_WEIGHT + 1, w2, record)


def _merge_kernel(*refs, with_router):
    if with_router:
        (of_ref, os_ref, od_ref, gl_ref, x_ref, wb_ref, wo_ref, g1_ref, gate_ref,
         g2_ref, sc_ref, sh_ref, r_ref, xo_ref, ho_ref, cw_ref) = refs
    else:
        (of_ref, os_ref, od_ref, gl_ref, x_ref, wb_ref, wo_ref, g1_ref, gate_ref,
         g2_ref, sc_ref, sh_ref, xo_ref, ho_ref) = refs
    merged = None
    for n, o_ref in enumerate((of_ref, os_ref, od_ref)):
        p = jnp.dot(o_ref[...], wb_ref[n], preferred_element_type=F32)
        gated = _sigmoid(gl_ref[:, n * D_MODEL:(n + 1) * D_MODEL].astype(F32)) * p
        merged = gated if merged is None else merged + gated
    y = jnp.dot(merged.astype(BF16), wo_ref[...], preferred_element_type=F32)
    xn = x_ref[...] + gate_ref[...] * _rms(y, g1_ref[...])
    xo_ref[...] = xn
    h = _rms(xn, g2_ref[...]) * (1.0 + sc_ref[...]) + sh_ref[...]
    ho_ref[...] = h.astype(ho_ref.dtype)
    if with_router:
        cw_ref[...] = _route(h, r_ref)


def _merge(o_fox, o_sb, o_diff, proj, x2d, w_branch, w_out, g1, g2, mod, mod_attn, mod_ffn,
           router_packed, seq):
    tokens = x2d.shape[0]
    tm = min(512, seq)
    tpb = seq // tm
    with_router = router_packed is not None
    row = lambda i: (i, 0)
    const2 = lambda i: (0, 0)
    in_specs = [
        pl.BlockSpec((tm, BRANCH_WIDTH), row),
        pl.BlockSpec((tm, BRANCH_WIDTH), row),
        pl.BlockSpec((tm, BRANCH_WIDTH), row),
        pl.BlockSpec((tm, GATE_COLS), row),
        pl.BlockSpec((tm, D_MODEL), row),
        pl.BlockSpec((N_BRANCHES, BRANCH_WIDTH, D_MODEL), lambda i: (0, 0, 0)),
        pl.BlockSpec((D_MODEL, D_MODEL), const2),
        pl.BlockSpec((1, D_MODEL), const2),
        _mod_spec(mod_attn, 2, tpb),
        pl.BlockSpec((1, D_MODEL), const2),
        _mod_spec(mod_ffn, 1, tpb),
        _mod_spec(mod_ffn, 0, tpb),
    ]
    args = [o_fox, o_sb, o_diff, proj, x2d, w_branch, w_out, g1, mod, g2, mod, mod]
    out_specs = [pl.BlockSpec((tm, D_MODEL), row), pl.BlockSpec((tm, D_MODEL), row)]
    out_shape = [jax.ShapeDtypeStruct((tokens, D_MODEL), F32),
                 jax.ShapeDtypeStruct((tokens, D_MODEL), F32 if with_router else BF16)]
    if with_router:
        in_specs.append(pl.BlockSpec((D_MODEL, LANES), const2))
        args.append(router_packed)
        out_specs.append(pl.BlockSpec((tm, LANES), row))
        out_shape.append(jax.ShapeDtypeStruct((tokens, LANES), F32))
    return pl.pallas_call(
        functools.partial(_merge_kernel, with_router=with_router),
        grid=(tokens // tm,),
        in_specs=in_specs,
        out_specs=out_specs,
        out_shape=out_shape,
        compiler_params=_cparams(("parallel",)),
        name="merge_router" if with_router else "merge",
    )(*args)


def _swiglu_partial(h, wg, wu):
    gate = jnp.dot(h, wg, preferred_element_type=F32)
    up = jnp.dot(h, wu, preferred_element_type=F32)
    return gate * _sigmoid(gate) * up


def _ffn_kernel(h_ref, wg_ref, wu_ref, wd_ref, x_ref, g_ref, gate_ref, o_ref, acc_ref):
    j = pl.program_id(1)

    @pl.when(j == 0)
    def _():
        acc_ref[...] = jnp.zeros_like(acc_ref)

    act = _swiglu_partial(h_ref[...], wg_ref[...], wu_ref[...])
    acc_ref[...] += jnp.dot(act.astype(BF16), wd_ref[...], preferred_element_type=F32)

    @pl.when(j == pl.num_programs(1) - 1)
    def _():
        o_ref[...] = x_ref[...] + gate_ref[...] * _rms(acc_ref[...], g_ref[...])


def _dense_ffn(h, x2d, wg, wu, wd, g, mod, mod_idx, seq):
    tokens = x2d.shape[0]
    tm = min(1024, seq)
    tf = 512
    tpb = seq // tm
    return pl.pallas_call(
        _ffn_kernel,
        grid=(tokens // tm, FFN_DIM // tf),
        in_specs=[
            pl.BlockSpec((tm, D_MODEL), lambda i, j: (i, 0)),
            pl.BlockSpec((D_MODEL, tf), lambda i, j: (0, j)),
            pl.BlockSpec((D_MODEL, tf), lambda i, j: (0, j)),
            pl.BlockSpec((tf, D_MODEL), lambda i, j: (j, 0)),
            pl.BlockSpec((tm, D_MODEL), lambda i, j: (i, 0)),
            pl.BlockSpec((1, D_MODEL), lambda i, j: (0, 0)),
            _mod_spec(mod_idx, 2, tpb),
        ],
        out_specs=pl.BlockSpec((tm, D_MODEL), lambda i, j: (i, 0)),
        out_shape=jax.ShapeDtypeStruct((tokens, D_MODEL), F32),
        scratch_shapes=[pltpu.VMEM((tm, D_MODEL), F32)],
        compiler_params=_cparams(("parallel", "arbitrary")),
        name="dense_ffn",
    )(h, wg, wu, wd, x2d, g, mod)


MOE_TILE = 512
RANK_BLOCK = 512


def _rank_kernel(route_ref, rank_ref, total_ref, count_scr):
    @pl.when(pl.program_id(0) == 0)
    def _():
        count_scr[...] = jnp.zeros_like(count_scr)

    route = route_ref[...]
    lane = lax.broadcasted_iota(jnp.int32, route.shape, 1)
    lane_f = lane.astype(F32)
    e1 = route[:, ROUTE_EXPERT:ROUTE_EXPERT + 1]
    e2 = route[:, ROUTE_EXPERT + 1:ROUTE_EXPERT + 2]
    member = jnp.where(lane_f == e1, 1.0, jnp.where(lane_f == e2, 1.0, 0.0))
    row = lax.broadcasted_iota(jnp.int32, (RANK_BLOCK, RANK_BLOCK), 0)
    col = lax.broadcasted_iota(jnp.int32, (RANK_BLOCK, RANK_BLOCK), 1)
    earlier = jnp.where(col < row, 1.0, 0.0).astype(BF16)
    before = jnp.dot(earlier, member.astype(BF16), preferred_element_type=F32) + count_scr[...]
    rank1 = jnp.sum(jnp.where(lane_f == e1, before, 0.0), axis=1, keepdims=True)
    rank2 = jnp.sum(jnp.where(lane_f == e2, before, 0.0), axis=1, keepdims=True)
    rank_ref[...] = jnp.where(lane == 0, rank1, jnp.where(lane == 1, rank2, 0.0))
    count_scr[...] += jnp.sum(member, axis=0, keepdims=True)
    total_ref[...] = count_scr[...]


def _expert_ranks(route):
    tokens = route.shape[0]
    return pl.pallas_call(
        _rank_kernel,
        grid=(tokens // RANK_BLOCK,),
        in_specs=[pl.BlockSpec((RANK_BLOCK, LANES), lambda i: (i, 0))],
        out_specs=[pl.BlockSpec((RANK_BLOCK, LANES), lambda i: (i, 0)),
                   pl.BlockSpec((1, LANES), lambda i: (0, 0))],
        out_shape=[jax.ShapeDtypeStruct((tokens, LANES), F32),
                   jax.ShapeDtypeStruct((1, LANES), F32)],
        scratch_shapes=[pltpu.VMEM((1, LANES), F32)],
        compiler_params=_cparams(("arbitrary",)),
        name="expert_ranks",
    )(route)


def _routing_tables(route, ranks, totals, n_tiles):
    counts = totals[0, :N_EXPERTS].astype(jnp.int32)
    padded = (counts + MOE_TILE - 1) // MOE_TILE * MOE_TILE
    ends = jnp.cumsum(padded)
    starts = ends - padded
    e1 = route[:, ROUTE_EXPERT].astype(jnp.int32)
    e2 = route[:, ROUTE_EXPERT + 1].astype(jnp.int32)
    pos1 = starts[e1] + ranks[:, 0].astype(jnp.int32)
    pos2 = starts[e2] + ranks[:, 1].astype(jnp.int32)
    n_used = ends[-1:] // MOE_TILE
    tile_start = jnp.minimum(jnp.arange(n_tiles, dtype=jnp.int32), n_used - 1) * MOE_TILE
    tile_expert = jnp.sum(tile_start[:, None] >= ends[None, :], axis=1).astype(jnp.int32)
    return pos1, pos2, tile_expert, n_used.astype(jnp.int32)


def _dispatch_kernel(pos1_ref, pos2_ref, h_ref, init_hbm, xs_hbm, sem, *, tm):
    del init_hbm
    base = pl.program_id(0) * tm

    def row_copy(r, pos_ref):
        return pltpu.make_async_copy(
            h_ref.at[pl.ds(r, 1)], xs_hbm.at[pl.ds(pos_ref[base + r], 1)], sem)

    def issue(r, _):
        row_copy(r, pos1_ref).start()
        row_copy(r, pos2_ref).start()
        return 0

    def drain(r, _):
        row_copy(r, pos1_ref).wait()
        row_copy(r, pos2_ref).wait()
        return 0

    lax.fori_loop(0, tm, issue, 0)
    lax.fori_loop(0, tm, drain, 0)


def _dispatch(h, pos1, pos2, n_rows):
    tokens = h.shape[0]
    tm = 512
    return pl.pallas_call(
        functools.partial(_dispatch_kernel, tm=tm),
        grid_spec=pltpu.PrefetchScalarGridSpec(
            num_scalar_prefetch=2,
            grid=(tokens // tm,),
            in_specs=[pl.BlockSpec((tm, D_MODEL), lambda i, p1, p2: (i, 0)),
                      pl.BlockSpec(memory_space=pl.ANY)],
            out_specs=pl.BlockSpec(memory_space=pl.ANY),
            scratch_shapes=[pltpu.SemaphoreType.DMA],
        ),
        out_shape=jax.ShapeDtypeStruct((n_rows, D_MODEL), F32),
        input_output_aliases={3: 0},
        compiler_params=_cparams(("arbitrary",)),
        name="expert_dispatch",
    )(pos1, pos2, h, jnp.zeros((n_rows, D_MODEL), F32))


def _grouped_ffn_kernel(te_ref, nu_ref, xs_ref, wg_ref, wu_ref, wd_ref, ys_ref, xb_scr, acc_scr):
    del te_ref
    i = pl.program_id(0)
    j = pl.program_id(1)
    last = pl.num_programs(1) - 1
    used = i < nu_ref[0]

    @pl.when(used & (j == 0))
    def _():
        xb_scr[...] = xs_ref[...].astype(BF16)
        acc_scr[...] = jnp.zeros_like(acc_scr)

    @pl.when(used)
    def _():
        act = _swiglu_partial(xb_scr[...], wg_ref[...], wu_ref[...])
        acc_scr[...] += jnp.dot(act.astype(BF16), wd_ref[...], preferred_element_type=F32)

    @pl.when(used & (j == last))
    def _():
        ys_ref[...] = acc_scr[...]

    @pl.when(jnp.logical_not(used) & (j == last))
    def _():
        ys_ref[...] = jnp.zeros_like(ys_ref)


def _grouped_ffn(xs, tile_expert, n_used, wg, wu, wd):
    n_rows = xs.shape[0]
    tf = 512
    nj = FFN_DIM // tf

    def ffn_col(i, j, nu):
        return jnp.where(i < nu[0], j, nj - 1)

    return pl.pallas_call(
        _grouped_ffn_kernel,
        grid_spec=pltpu.PrefetchScalarGridSpec(
            num_scalar_prefetch=2,
            grid=(n_rows // MOE_TILE, nj),
            in_specs=[
                pl.BlockSpec((MOE_TILE, D_MODEL),
                             lambda i, j, te, nu: (jnp.minimum(i, nu[0] - 1), 0)),
                pl.BlockSpec((None, D_MODEL, tf),
                             lambda i, j, te, nu: (te[i], 0, ffn_col(i, j, nu))),
                pl.BlockSpec((None, D_MODEL, tf),
                             lambda i, j, te, nu: (te[i], 0, ffn_col(i, j, nu))),
                pl.BlockSpec((None, tf, D_MODEL),
                             lambda i, j, te, nu: (te[i], ffn_col(i, j, nu), 0)),
            ],
            out_specs=pl.BlockSpec((MOE_TILE, D_MODEL), lambda i, j, te, nu: (i, 0)),
            scratch_shapes=[pltpu.VMEM((MOE_TILE, D_MODEL), BF16),
                            pltpu.VMEM((MOE_TILE, D_MODEL), F32)],
        ),
        out_shape=jax.ShapeDtypeStruct((n_rows, D_MODEL), F32),
        compiler_params=_cparams(("arbitrary", "arbitrary")),
        name="grouped_ffn",
    )(tile_expert, n_used, xs, wg, wu, wd)


def _combine_kernel(pos1_ref, pos2_ref, ys_hbm, route_ref, x_ref, g_ref, gate_ref, o_ref,
                    y1_scr, y2_scr, sem, *, tm):
    base = pl.program_id(0) * tm

    def row_copy(r, pos_ref, dst):
        return pltpu.make_async_copy(
            ys_hbm.at[pl.ds(pos_ref[base + r], 1)], dst.at[pl.ds(r, 1)], sem)

    def issue(r, _):
        row_copy(r, pos1_ref, y1_scr).start()
        row_copy(r, pos2_ref, y2_scr).start()
        return 0

    def drain(r, _):
        row_copy(r, pos1_ref, y1_scr).wait()
        row_copy(r, pos2_ref, y2_scr).wait()
        return 0

    lax.fori_loop(0, tm, issue, 0)
    lax.fori_loop(0, tm, drain, 0)
    route = route_ref[...]
    w1 = route[:, ROUTE_WEIGHT:ROUTE_WEIGHT + 1]
    w2 = route[:, ROUTE_WEIGHT + 1:ROUTE_WEIGHT + 2]
    y = w1 * y1_scr[...] + w2 * y2_scr[...]
    o_ref[...] = x_ref[...] + gate_ref[...] * _rms(y, g_ref[...])


def _combine(ys, pos1, pos2, route, x2d, g, mod, mod_idx, seq):
    tokens = x2d.shape[0]
    tm = min(256, seq)
    tpb = seq // tm
    gate_spec = _mod_spec(mod_idx, 2, tpb)
    return pl.pallas_call(
        functools.partial(_combine_kernel, tm=tm),
        grid_spec=pltpu.PrefetchScalarGridSpec(
            num_scalar_prefetch=2,
            grid=(tokens // tm,),
            in_specs=[
                pl.BlockSpec(memory_space=pl.ANY),
                pl.BlockSpec((tm, LANES), lambda i, p1, p2: (i, 0)),
                pl.BlockSpec((tm, D_MODEL), lambda i, p1, p2: (i, 0)),
                pl.BlockSpec((1, D_MODEL), lambda i, p1, p2: (0, 0)),
                pl.BlockSpec(gate_spec.block_shape, lambda i, p1, p2: gate_spec.index_map(i)),
            ],
            out_specs=pl.BlockSpec((tm, D_MODEL), lambda i, p1, p2: (i, 0)),
            scratch_shapes=[pltpu.VMEM((tm, D_MODEL), F32), pltpu.VMEM((tm, D_MODEL), F32),
                            pltpu.SemaphoreType.DMA],
        ),
        out_shape=jax.ShapeDtypeStruct((tokens, D_MODEL), F32),
        compiler_params=_cparams(("arbitrary",)),
        name="expert_combine",
    )(pos1, pos2, ys, route, x2d, g, mod)


def _moe_ffn(h, route, x2d, wg, wu, wd, g, mod, mod_idx, seq):
    tokens = x2d.shape[0]
    n_rows = 2 * tokens + N_EXPERTS * MOE_TILE
    ranks, totals = _expert_ranks(route)
    pos1, pos2, tile_expert, n_used = _routing_tables(route, ranks, totals, n_rows // MOE_TILE)
    xs = _dispatch(h, pos1, pos2, n_rows)
    ys = _grouped_ffn(xs, tile_expert, n_used, wg, wu, wd)
    return _combine(ys, pos1, pos2, route, x2d, g, mod, mod_idx, seq)


def _prep_w_in(w):
    fw = FOX_HEADS * HEAD_DIM
    sw = SB_HEADS * HEAD_DIM
    dw = DIFF_HEADS * 2 * HEAD_DIM
    sizes = (fw, fw, fw, FOX_HEADS, sw, sw, sw, dw, dw, dw, GATE_COLS)
    offs = [0]
    for s in sizes:
        offs.append(offs[-1] + s)
    part = [w[:, offs[i]:offs[i + 1]] for i in range(len(sizes))]
    fq, fk, fv, ff, sq, sk, sv, dq, dk, dv, gl = part
    w_main = jnp.concatenate([gl, fq, fk, fv, sq, sk, sv, dq, dk, dv], axis=1).astype(BF16)
    w_forget = jnp.pad(ff, ((0, 0), (0, LANES - FOX_HEADS))).astype(BF16)
    return w_main, w_forget


def _pack_router(r):
    p1 = r.astype(BF16)
    r1 = r - p1.astype(F32)
    p2 = r1.astype(BF16)
    p3 = (r1 - p2.astype(F32)).astype(BF16)
    packed = jnp.concatenate([p1, p2, p3], axis=1)
    return jnp.pad(packed, ((0, 0), (0, LANES - 3 * N_EXPERTS)))


def kernel(x, c, w_ada, b_ada, norm_g, w_in, fox_f_bias, diff_lambda, diff_subln_g, w_branch,
           w_out, ffn_w_gate, ffn_w_up, ffn_w_down, moe_router, moe_w_gate, moe_w_up,
           moe_w_down):
    batch, seq, _ = x.shape
    depth = w_in.shape[0]
    tokens = batch * seq
    x2d = x.reshape(tokens, D_MODEL)
    mod = _ada_modulation(c, w_ada, b_ada)
    slopes = jnp.exp2(-ALIBI_MAX_BIAS * jnp.arange(1, DIFF_HEADS + 1, dtype=F32) / DIFF_HEADS)
    slopes = jnp.broadcast_to(slopes[:, None, None], (DIFF_HEADS, 1, LANES))

    for layer in range(depth):
        lam_init = 0.8 - 0.6 * math.exp(-0.3 * layer)
        mod_attn, mod_ffn = 2 * layer, 2 * layer + 1
        w_main, w_forget = _prep_w_in(w_in[layer])
        proj, forget_logits = _in_projection(
            x2d, norm_g[layer, 0][None, :], mod, mod_attn, w_main, w_forget, seq)
        key_bias = _fox_decay(forget_logits, fox_f_bias[layer], batch, seq)

        o_fox = _attention_call(
            _fox_kernel, proj, (COLBLK_FQ, COLBLK_FK, COLBLK_FV), [key_bias],
            [pl.BlockSpec((seq, LANES), lambda b, c: (b, c))],
            [_values_t_scratch(seq), _values_t_scratch(seq)],
            batch, seq, FOX_HEADS // 2, "fox_attention")
        o_sb = _attention_call(
            _sb_kernel, proj, (COLBLK_SQ, COLBLK_SK, COLBLK_SV), [], [],
            [_values_t_scratch(seq), _values_t_scratch(seq)],
            batch, seq, SB_HEADS // 2, "stickbreak_attention")
        o_diff = _attention_call(
            functools.partial(_diff_kernel, lam_init=lam_init), proj,
            (COLBLK_DQ, COLBLK_DK, COLBLK_DV),
            [slopes, diff_lambda[layer].astype(F32), diff_subln_g[layer][None, :].astype(F32)],
            [pl.BlockSpec((None, 1, LANES), lambda b, c: (c, 0, 0)),
             pl.BlockSpec((4, HEAD_DIM), lambda b, c: (0, 0)),
             pl.BlockSpec((1, 2 * HEAD_DIM), lambda b, c: (0, 0))],
            [_values_t_scratch(seq), pltpu.VMEM((seq, LANES), BF16)],
            batch, seq, DIFF_HEADS, "diff_attention")

        is_moe = layer % 2 == 1
        idx = layer // 2
        router_packed = _pack_router(moe_router[idx]) if is_moe else None
        merged = _merge(
            o_fox, o_sb, o_diff, proj, x2d, w_branch[layer].astype(BF16),
            w_out[layer].astype(BF16), norm_g[layer, 1][None, :], norm_g[layer, 2][None, :],
            mod, mod_attn, mod_ffn, router_packed, seq)
        g3 = norm_g[layer, 3][None, :]
        if is_moe:
            x2d, h, combine = merged
            x2d = _moe_ffn(h, combine, x2d, moe_w_gate[idx].astype(BF16),
                           moe_w_up[idx].astype(BF16), moe_w_down[idx].astype(BF16),
                           g3, mod, mod_ffn, seq)
        else:
            x2d, h = merged
            x2d = _dense_ffn(h, x2d, ffn_w_gate[idx].astype(BF16), ffn_w_up[idx].astype(BF16),
                             ffn_w_down[idx].astype(BF16), g3, mod, mod_ffn, seq)
    return x2d.reshape(batch, seq, D_MODEL)
```

```python
import functools
import math

import jax
import jax.numpy as jnp
import numpy as np
from jax import lax
from jax.experimental import pallas as pl
from jax.experimental.pallas import tpu as pltpu

F32 = jnp.float32
BF16 = jnp.bfloat16

D_MODEL = 1024
HEAD_DIM = 64
FOX_HEADS = 8
SB_HEADS = 8
DIFF_HEADS = 4
BRANCH_WIDTH = 512
N_BRANCHES = 3
FFN_DIM = 3584
N_EXPERTS = 8
RMS_EPS = 1e-6
ALIBI_MAX_BIAS = 8.0

LANES = 128
GATE_COLS = N_BRANCHES * D_MODEL
COLBLK_FQ = GATE_COLS // LANES
COLBLK_FK = COLBLK_FQ + 4
COLBLK_FV = COLBLK_FK + 4
COLBLK_SQ = COLBLK_FV + 4
COLBLK_SK = COLBLK_SQ + 4
COLBLK_SV = COLBLK_SK + 4
COLBLK_DQ = COLBLK_SV + 4
COLBLK_DK = COLBLK_DQ + 4
COLBLK_DV = COLBLK_DK + 4
PROJ_COLS = (COLBLK_DV + 4) * LANES

VMEM_LIMIT = 56 * 1024 * 1024
NEG_BIG = -1e30
QK_SCALE = HEAD_DIM ** -0.5


def _cparams(sem):
    return pltpu.CompilerParams(dimension_semantics=sem, vmem_limit_bytes=VMEM_LIMIT)


def _sigmoid(v):
    return 1.0 / (1.0 + jnp.exp(-v))


def _rms(v, g):
    ms = jnp.mean(v * v, axis=-1, keepdims=True)
    return v * lax.rsqrt(ms + RMS_EPS) * g


def _ada_kernel(c_ref, w_ref, b_ref, o_ref):
    c = c_ref[...]
    a = (c * _sigmoid(c)).astype(BF16)
    o_ref[...] = jnp.dot(a, w_ref[...].astype(BF16), preferred_element_type=F32) + b_ref[...]


def _ada_modulation(c, w_ada, b_ada):
    n_mod = w_ada.shape[0] * w_ada.shape[1]
    batch = c.shape[0]
    w = w_ada.reshape(n_mod, D_MODEL, 3 * D_MODEL)
    b = b_ada.reshape(n_mod, 1, 3 * D_MODEL)
    out = pl.pallas_call(
        _ada_kernel,
        grid=(n_mod, 3),
        in_specs=[
            pl.BlockSpec((batch, D_MODEL), lambda m, j: (0, 0)),
            pl.BlockSpec((None, D_MODEL, D_MODEL), lambda m, j: (m, 0, j)),
            pl.BlockSpec((None, 1, D_MODEL), lambda m, j: (m, 0, j)),
        ],
        out_specs=pl.BlockSpec((None, batch, D_MODEL), lambda m, j: (m, 0, j)),
        out_shape=jax.ShapeDtypeStruct((n_mod, batch, 3 * D_MODEL), F32),
        compiler_params=_cparams(("parallel", "parallel")),
        name="ada_modulation",
    )(c, w, b)
    return out.reshape(n_mod, batch, 1, 3 * D_MODEL)


def _mod_spec(mod_idx, part, rows_per_batch_tiles):
    def index(i, *_):
        return (mod_idx, i // rows_per_batch_tiles, 0, part)
    return pl.BlockSpec((None, None, 1, D_MODEL), index)


def _inproj_kernel(x_ref, g_ref, sc_ref, sh_ref, w_ref, wf_ref, o_ref, f_ref, h_scr):
    @pl.when(pl.program_id(1) == 0)
    def _():
        h = _rms(x_ref[...], g_ref[...]) * (1.0 + sc_ref[...]) + sh_ref[...]
        hb = h.astype(BF16)
        h_scr[...] = hb
        f_ref[...] = jnp.dot(hb, wf_ref[...], preferred_element_type=F32)

    o_ref[...] = jnp.dot(h_scr[...], w_ref[...], preferred_element_type=F32).astype(BF16)


def _in_projection(x2d, g, mod, mod_idx, w_main, w_forget, seq):
    tokens = x2d.shape[0]
    tm = min(1024, seq)
    tn = 1536
    tiles_per_batch = seq // tm
    return pl.pallas_call(
        _inproj_kernel,
        grid=(tokens // tm, PROJ_COLS // tn),
        in_specs=[
            pl.BlockSpec((tm, D_MODEL), lambda i, j: (i, 0)),
            pl.BlockSpec((1, D_MODEL), lambda i, j: (0, 0)),
            _mod_spec(mod_idx, 1, tiles_per_batch),
            _mod_spec(mod_idx, 0, tiles_per_batch),
            pl.BlockSpec((D_MODEL, tn), lambda i, j: (0, j)),
            pl.BlockSpec((D_MODEL, LANES), lambda i, j: (0, 0)),
        ],
        out_specs=[
            pl.BlockSpec((tm, tn), lambda i, j: (i, j)),
            pl.BlockSpec((tm, LANES), lambda i, j: (i, 0)),
        ],
        out_shape=[
            jax.ShapeDtypeStruct((tokens, PROJ_COLS), BF16),
            jax.ShapeDtypeStruct((tokens, LANES), F32),
        ],
        scratch_shapes=[pltpu.VMEM((tm, D_MODEL), BF16)],
        compiler_params=_cparams(("parallel", "arbitrary")),
        name="in_projection",
    )(x2d, g, mod, mod, w_main, w_forget)


CUMSUM_BLOCK = 256
BIAS_PIECES = 3


def _split3(v):
    p1 = v.astype(BF16).astype(F32)
    r1 = v - p1
    p2 = r1.astype(BF16).astype(F32)
    p3 = (r1 - p2).astype(BF16).astype(F32)
    return p1, p2, p3


def _decay_kernel(f_ref, b_ref, sel_ref, o_ref):
    seq = f_ref.shape[0]
    row = lax.broadcasted_iota(jnp.int32, (CUMSUM_BLOCK, CUMSUM_BLOCK), 0)
    col = lax.broadcasted_iota(jnp.int32, (CUMSUM_BLOCK, CUMSUM_BLOCK), 1)
    tri = jnp.where(col <= row, 1.0, 0.0).astype(BF16)
    carry = jnp.zeros((1, LANES), F32)
    for blk in range(seq // CUMSUM_BLOCK):
        rows = slice(blk * CUMSUM_BLOCK, (blk + 1) * CUMSUM_BLOCK)
        z = f_ref[rows, :] + b_ref[...]
        logf = jnp.minimum(z, 0.0) - jnp.log1p(jnp.exp(-jnp.abs(z)))
        cum = carry
        for piece in _split3(logf):
            cum = cum + jnp.dot(tri, piece.astype(BF16), preferred_element_type=F32)
        carry = cum[CUMSUM_BLOCK - 1:CUMSUM_BLOCK, :]
        bias = None
        for i, piece in enumerate(_split3(-cum)):
            part = jnp.dot(piece.astype(BF16), sel_ref[i], preferred_element_type=F32)
            bias = part if bias is None else bias + part
        o_ref[rows, :] = bias.astype(BF16)


def _decay_selectors():
    sel = np.zeros((BIAS_PIECES, LANES, FOX_HEADS // 2 * LANES), np.float32)
    for h in range(FOX_HEADS):
        for i in range(BIAS_PIECES):
            sel[i, h, (h // 2) * LANES + (h % 2) * BIAS_PIECES + i] = 1.0
    return jnp.asarray(sel, BF16)


def _fox_decay(forget_logits, bias, batch, seq):
    bias_row = jnp.zeros((1, LANES), F32).at[0, :FOX_HEADS].set(bias.astype(F32))
    width = FOX_HEADS // 2 * LANES
    return pl.pallas_call(
        _decay_kernel,
        grid=(batch,),
        in_specs=[
            pl.BlockSpec((seq, LANES), lambda b: (b, 0)),
            pl.BlockSpec((1, LANES), lambda b: (0, 0)),
            pl.BlockSpec((BIAS_PIECES, LANES, width), lambda b: (0, 0, 0)),
        ],
        out_specs=pl.BlockSpec((seq, width), lambda b: (b, 0)),
        out_shape=jax.ShapeDtypeStruct((batch * seq, width), BF16),
        compiler_params=_cparams(("parallel",)),
        name="fox_decay",
    )(forget_logits, bias_row, _decay_selectors())


ATT_BLOCK = 256


def _transposed_queries(q):
    return q.astype(F32).T.astype(BF16)


def _head_halves(x, axis):
    low = lax.broadcasted_iota(jnp.int32, x.shape, axis) < HEAD_DIM
    zero = jnp.zeros_like(x)
    return jnp.where(low, x, zero), jnp.where(low, zero, x)


def _key_query_masks(t):
    key = lax.broadcasted_iota(jnp.int32, (t, t), 0)
    query = lax.broadcasted_iota(jnp.int32, (t, t), 1)
    return key <= query, key < query


def _bias_ones(first_lane):
    lane = lax.broadcasted_iota(jnp.int32, (ATT_BLOCK, LANES), 1)
    hit = (lane >= first_lane) & (lane < first_lane + BIAS_PIECES)
    return jnp.where(hit, 1.0, 0.0).astype(BF16)


def _store_values_t(v_ref, vt_refs):
    t = ATT_BLOCK
    for blk in range(v_ref.shape[0] // t):
        vt = v_ref[blk * t:(blk + 1) * t, :].astype(F32).T
        if len(vt_refs) == 2:
            va, vb = _head_halves(vt, 0)
            vt_refs[0][blk] = va.astype(BF16)
            vt_refs[1][blk] = vb.astype(BF16)
        else:
            vt_refs[0][blk] = vt.astype(BF16)


def _softmax_step_t(s, m, l):
    m_new = jnp.maximum(m, jnp.max(s, axis=0, keepdims=True))
    alpha = jnp.exp(m - m_new)
    p = jnp.exp(s - m_new)
    l = alpha * l + jnp.sum(p, axis=0, keepdims=True)
    return m_new, l, alpha, p.astype(BF16)


def _stat_init(t):
    return jnp.full((1, t), NEG_BIG, F32), jnp.zeros((1, t), F32)


def _fox_kernel(q_ref, k_ref, v_ref, kb_ref, o_ref, vta_ref, vtb_ref, s_scr, p_scr, acc_scr):
    t = ATT_BLOCK
    n_blocks = q_ref.shape[0] // t
    _store_values_t(v_ref, (vta_ref, vtb_ref))
    causal, _ = _key_query_masks(t)
    ones_a, ones_b = _bias_ones(0), _bias_ones(BIAS_PIECES)
    head_a_rows = lax.broadcasted_iota(jnp.int32, (LANES, t), 0) < HEAD_DIM

    def q_block(qi, _):
        q0 = pl.multiple_of(qi * t, t)
        qa, qb = _head_halves(q_ref[pl.ds(q0, t), :] * QK_SCALE, 1)
        qta = _transposed_queries(jnp.concatenate([qa, ones_a], axis=1))
        qtb = _transposed_queries(jnp.concatenate([qb, ones_b], axis=1))

        def scores(j):
            k0 = pl.multiple_of(j * t, t)
            k = jnp.concatenate([k_ref[pl.ds(k0, t), :], kb_ref[pl.ds(k0, t), :]], axis=1)
            return (jnp.dot(k, qta, preferred_element_type=F32),
                    jnp.dot(k, qtb, preferred_element_type=F32))

        def accumulate(j, pa, pb, alpha_a, alpha_b):
            acc_scr[...] = (jnp.where(head_a_rows, alpha_a, alpha_b) * acc_scr[...]
                            + jnp.dot(vta_ref[j], pa, preferred_element_type=F32)
                            + jnp.dot(vtb_ref[j], pb, preferred_element_type=F32))

        def stage(s_next, pa, pb):
            s_scr[0], s_scr[1] = s_next
            p_scr[0], p_scr[1] = pa, pb

        def trip(j, c):
            alpha_a, alpha_b, ma, la, mb, lb = c
            s_next = scores(j + 1)
            accumulate(jnp.maximum(j - 1, 0), p_scr[0], p_scr[1], alpha_a, alpha_b)
            ma, la, alpha_a, pa = _softmax_step_t(s_scr[0], ma, la)
            mb, lb, alpha_b, pb = _softmax_step_t(s_scr[1], mb, lb)
            stage(s_next, pa, pb)
            return alpha_a, alpha_b, ma, la, mb, lb

        no_p = jnp.zeros((t, t), BF16)
        one_row = jnp.ones((1, t), F32)
        stage(scores(0), no_p, no_p)
        acc_scr[...] = jnp.zeros_like(acc_scr)
        alpha_a, alpha_b, ma, la, mb, lb = lax.fori_loop(
            0, qi, trip, (one_row, one_row) + _stat_init(t) + _stat_init(t))
        accumulate(jnp.maximum(qi - 1, 0), p_scr[0], p_scr[1], alpha_a, alpha_b)
        _, la, alpha_a, pa = _softmax_step_t(jnp.where(causal, s_scr[0], NEG_BIG), ma, la)
        _, lb, alpha_b, pb = _softmax_step_t(jnp.where(causal, s_scr[1], NEG_BIG), mb, lb)
        accumulate(qi, pa, pb, alpha_a, alpha_b)
        o_t = acc_scr[...] / jnp.where(head_a_rows, la, lb)
        o_ref[pl.ds(q0, t), :] = o_t.T.astype(o_ref.dtype)
        return 0

    lax.fori_loop(0, n_blocks, q_block, 0)


def _sb_kernel(q_ref, k_ref, v_ref, o_ref, vta_ref, vtb_ref, z_scr, a_scr, acc_scr):
    t = ATT_BLOCK
    n_blocks = q_ref.shape[0] // t
    _store_values_t(v_ref, (vta_ref, vtb_ref))
    _, strict = _key_query_masks(t)
    row = lax.broadcasted_iota(jnp.int32, (t, t), 0)
    col = lax.broadcasted_iota(jnp.int32, (t, t), 1)
    later_keys = jnp.where(col > row, 1.0, 0.0).astype(BF16)

    def q_block(qi, _):
        q0 = pl.multiple_of(qi * t, t)
        qa, qb = _head_halves(q_ref[pl.ds(q0, t), :] * QK_SCALE, 1)
        qta, qtb = _transposed_queries(qa), _transposed_queries(qb)

        def scores(step):
            j = jnp.maximum(qi - step, 0)
            k = k_ref[pl.ds(pl.multiple_of(j * t, t), t), :]
            return (jnp.dot(k, qta, preferred_element_type=F32),
                    jnp.dot(k, qtb, preferred_element_type=F32))

        def weights(z, later, diagonal):
            softplus = jnp.maximum(z, 0.0) + jnp.log(1.0 + jnp.exp(-jnp.abs(z)))
            log_stay = -softplus
            if diagonal:
                log_stay = jnp.where(strict, log_stay, 0.0)
            tail = jnp.dot(later_keys, log_stay.astype(BF16),
                           preferred_element_type=F32) + later
            a = jnp.exp(z + log_stay + tail)
            if diagonal:
                a = jnp.where(strict, a, 0.0)
            return a.astype(BF16), later + jnp.sum(log_stay, axis=0, keepdims=True)

        def accumulate(step, aa, ab):
            j = qi - step
            acc_scr[...] += (jnp.dot(vta_ref[j], aa, preferred_element_type=F32)
                             + jnp.dot(vtb_ref[j], ab, preferred_element_type=F32))

        def stage(z_next, aa, ab):
            z_scr[0], z_scr[1] = z_next
            a_scr[0], a_scr[1] = aa, ab

        def trip(step, c):
            later_a, later_b = c
            z_next = scores(step + 1)
            accumulate(step - 1, a_scr[0], a_scr[1])
            aa, later_a = weights(z_scr[0], later_a, False)
            ab, later_b = weights(z_scr[1], later_b, False)
            stage(z_next, aa, ab)
            return later_a, later_b

        za, zb = scores(0)
        z_next = scores(1)
        zero_row = jnp.zeros((1, t), F32)
        aa, later_a = weights(za, zero_row, True)
        ab, later_b = weights(zb, zero_row, True)
        stage(z_next, aa, ab)
        acc_scr[...] = jnp.zeros_like(acc_scr)
        lax.fori_loop(1, qi + 1, trip, (later_a, later_b))
        accumulate(qi, a_scr[0], a_scr[1])
        o_ref[pl.ds(q0, t), :] = acc_scr[...].T.astype(o_ref.dtype)
        return 0

    lax.fori_loop(0, n_blocks, q_block, 0)


def _diff_kernel(q_ref, k_ref, v_ref, slope_ref, dl_ref, g_ref, o_ref, vt_ref, kb_ref,
                 s_scr, p_scr, acc_scr, *, lam_init):
    t = ATT_BLOCK
    n_blocks = q_ref.shape[0] // t
    _store_values_t(v_ref, (vt_ref,))
    slope = slope_ref[:, 0:1]
    lane = lax.broadcasted_iota(jnp.int32, (t, LANES), 1)
    for blk in range(n_blocks):
        pos = (lax.broadcasted_iota(jnp.int32, (t, LANES), 0) + blk * t).astype(F32)
        p1, p2, p3 = _split3(slope * pos)
        bias = jnp.where(lane == 0, p1, jnp.where(lane == 1, p2, jnp.where(lane == 2, p3, 0.0)))
        kb_ref[blk * t:(blk + 1) * t, :] = bias.astype(BF16)
    causal, _ = _key_query_masks(t)
    ones = _bias_ones(0)
    dl = dl_ref[...]
    lam = (jnp.exp(jnp.sum(dl[0:1] * dl[1:2], axis=1, keepdims=True))
           - jnp.exp(jnp.sum(dl[2:3] * dl[3:4], axis=1, keepdims=True)) + lam_init)

    def q_block(qi, _):
        q0 = pl.multiple_of(qi * t, t)
        q1, q2 = _head_halves(q_ref[pl.ds(q0, t), :] * QK_SCALE, 1)
        qt1 = _transposed_queries(jnp.concatenate([q1, ones], axis=1))
        qt2 = _transposed_queries(jnp.concatenate([q2, ones], axis=1))

        def scores(j):
            k0 = pl.multiple_of(j * t, t)
            k = jnp.concatenate([k_ref[pl.ds(k0, t), :], kb_ref[pl.ds(k0, t), :]], axis=1)
            return (jnp.dot(k, qt1, preferred_element_type=F32),
                    jnp.dot(k, qt2, preferred_element_type=F32))

        def accumulate(j, p1, p2, alpha1, alpha2):
            vt = vt_ref[j]
            acc_scr[0] = alpha1 * acc_scr[0] + jnp.dot(vt, p1, preferred_element_type=F32)
            acc_scr[1] = alpha2 * acc_scr[1] + jnp.dot(vt, p2, preferred_element_type=F32)

        def stage(s_next, p1, p2):
            s_scr[0], s_scr[1] = s_next
            p_scr[0], p_scr[1] = p1, p2

        def trip(j, c):
            alpha1, alpha2, m1, l1, m2, l2 = c
            s_next = scores(j + 1)
            accumulate(jnp.maximum(j - 1, 0), p_scr[0], p_scr[1], alpha1, alpha2)
            m1, l1, alpha1, p1 = _softmax_step_t(s_scr[0], m1, l1)
            m2, l2, alpha2, p2 = _softmax_step_t(s_scr[1], m2, l2)
            stage(s_next, p1, p2)
            return alpha1, alpha2, m1, l1, m2, l2

        no_p = jnp.zeros((t, t), BF16)
        one_row = jnp.ones((1, t), F32)
        stage(scores(0), no_p, no_p)
        acc_scr[...] = jnp.zeros_like(acc_scr)
        alpha1, alpha2, m1, l1, m2, l2 = lax.fori_loop(
            0, qi, trip, (one_row, one_row) + _stat_init(t) + _stat_init(t))
        accumulate(jnp.maximum(qi - 1, 0), p_scr[0], p_scr[1], alpha1, alpha2)
        _, l1, alpha1, p1 = _softmax_step_t(jnp.where(causal, s_scr[0], NEG_BIG), m1, l1)
        _, l2, alpha2, p2 = _softmax_step_t(jnp.where(causal, s_scr[1], NEG_BIG), m2, l2)
        accumulate(qi, p1, p2, alpha1, alpha2)
        o = (acc_scr[0] / l1 - lam * (acc_scr[1] / l2)).T
        o_ref[pl.ds(q0, t), :] = (_rms(o, g_ref[...]) * (1.0 - lam_init)).astype(o_ref.dtype)
        return 0

    lax.fori_loop(0, n_blocks, q_block, 0)


def _attention_call(kernel, proj, colblks, extra_in, extra_specs, scratch, batch, seq,
                    n_colblk, name):
    qc, kc, vc = colblks
    return pl.pallas_call(
        kernel,
        grid=(batch, n_colblk),
        in_specs=[
            pl.BlockSpec((seq, LANES), lambda b, c: (b, qc + c)),
            pl.BlockSpec((seq, LANES), lambda b, c: (b, kc + c)),
            pl.BlockSpec((seq, LANES), lambda b, c: (b, vc + c)),
        ] + extra_specs,
        out_specs=pl.BlockSpec((seq, LANES), lambda b, c: (b, c)),
        out_shape=jax.ShapeDtypeStruct((batch * seq, n_colblk * LANES), BF16),
        scratch_shapes=scratch,
        compiler_params=_cparams(("parallel", "parallel")),
        name=name,
    )(proj, proj, proj, *extra_in)


def _values_t_scratch(seq):
    return pltpu.VMEM((seq // ATT_BLOCK, LANES, ATT_BLOCK), BF16)


def _pipeline_scratch(n_acc):
    acc_shape = (LANES, ATT_BLOCK) if n_acc == 1 else (n_acc, LANES, ATT_BLOCK)
    return [pltpu.VMEM((2, ATT_BLOCK, ATT_BLOCK), F32),
            pltpu.VMEM((2, ATT_BLOCK, ATT_BLOCK), BF16),
            pltpu.VMEM(acc_shape, F32)]


ROUTE_EXPERT = 0
ROUTE_WEIGHT = 2


def _route(h, r_ref):
    h1 = h.astype(BF16)
    h2 = (h - h1.astype(F32)).astype(BF16)
    prod = (jnp.dot(h1, r_ref[...], preferred_element_type=F32)
            + jnp.dot(h2, r_ref[...], preferred_element_type=F32))
    lane = lax.broadcasted_iota(jnp.int32, prod.shape, 1)
    shifted1 = pltpu.roll(prod, LANES - N_EXPERTS, axis=1)
    shifted2 = pltpu.roll(prod, LANES - 2 * N_EXPERTS, axis=1)
    logits = jnp.where(lane < N_EXPERTS, prod + shifted1 + shifted2, NEG_BIG)
    v1 = jnp.max(logits, axis=1, keepdims=True)
    i1 = jnp.min(jnp.where(logits == v1, lane, LANES), axis=1, keepdims=True)
    rest = jnp.where(lane == i1, NEG_BIG, logits)
    v2 = jnp.max(rest, axis=1, keepdims=True)
    i2 = jnp.min(jnp.where(rest == v2, lane, LANES), axis=1, keepdims=True)
    e2 = jnp.exp(v2 - v1)
    w1 = 1.0 / (1.0 + e2)
    w2 = e2 / (1.0 + e2)
    record = jnp.where(lane == ROUTE_EXPERT, i1.astype(F32), 0.0)
    record = jnp.where(lane == ROUTE_EXPERT + 1, i2.astype(F32), record)
    record = jnp.where(lane == ROUTE_WEIGHT, w1, record)
    return jnp.where(lane == ROUTE_WEIGHT + 1, w2, record)


def _merge_kernel(*refs, with_router):
    if with_router:
        (of_ref, os_ref, od_ref, gl_ref, x_ref, wb_ref, wo_ref, g1_ref, gate_ref,
         g2_ref, sc_ref, sh_ref, r_ref, xo_ref, ho_ref, cw_ref) = refs
    else:
        (of_ref, os_ref, od_ref, gl_ref, x_ref, wb_ref, wo_ref, g1_ref, gate_ref,
         g2_ref, sc_ref, sh_ref, xo_ref, ho_ref) = refs
    merged = None
    for n, o_ref in enumerate((of_ref, os_ref, od_ref)):
        p = jnp.dot(o_ref[...], wb_ref[n], preferred_element_type=F32)
        gated = _sigmoid(gl_ref[:, n * D_MODEL:(n + 1) * D_MODEL].astype(F32)) * p
        merged = gated if merged is None else merged + gated
    y = jnp.dot(merged.astype(BF16), wo_ref[...], preferred_element_type=F32)
    xn = x_ref[...] + gate_ref[...] * _rms(y, g1_ref[...])
    xo_ref[...] = xn
    h = _rms(xn, g2_ref[...]) * (1.0 + sc_ref[...]) + sh_ref[...]
    ho_ref[...] = h.astype(ho_ref.dtype)
    if with_router:
        cw_ref[...] = _route(h, r_ref)


def _merge(o_fox, o_sb, o_diff, proj, x2d, w_branch, w_out, g1, g2, mod, mod_attn, mod_ffn,
           router_packed, seq):
    tokens = x2d.shape[0]
    tm = min(512, seq)
    tpb = seq // tm
    with_router = router_packed is not None
    row = lambda i: (i, 0)
    const2 = lambda i: (0, 0)
    in_specs = [
        pl.BlockSpec((tm, BRANCH_WIDTH), row),
        pl.BlockSpec((tm, BRANCH_WIDTH), row),
        pl.BlockSpec((tm, BRANCH_WIDTH), row),
        pl.BlockSpec((tm, GATE_COLS), row),
        pl.BlockSpec((tm, D_MODEL), row),
        pl.BlockSpec((N_BRANCHES, BRANCH_WIDTH, D_MODEL), lambda i: (0, 0, 0)),
        pl.BlockSpec((D_MODEL, D_MODEL), const2),
        pl.BlockSpec((1, D_MODEL), const2),
        _mod_spec(mod_attn, 2, tpb),
        pl.BlockSpec((1, D_MODEL), const2),
        _mod_spec(mod_ffn, 1, tpb),
        _mod_spec(mod_ffn, 0, tpb),
    ]
    args = [o_fox, o_sb, o_diff, proj, x2d, w_branch, w_out, g1, mod, g2, mod, mod]
    out_specs = [pl.BlockSpec((tm, D_MODEL), row), pl.BlockSpec((tm, D_MODEL), row)]
    out_shape = [jax.ShapeDtypeStruct((tokens, D_MODEL), F32),
                 jax.ShapeDtypeStruct((tokens, D_MODEL), F32 if with_router else BF16)]
    if with_router:
        in_specs.append(pl.BlockSpec((D_MODEL, LANES), const2))
        args.append(router_packed)
        out_specs.append(pl.BlockSpec((tm, LANES), row))
        out_shape.append(jax.ShapeDtypeStruct((tokens, LANES), F32))
    return pl.pallas_call(
        functools.partial(_merge_kernel, with_router=with_router),
        grid=(tokens // tm,),
        in_specs=in_specs,
        out_specs=out_specs,
        out_shape=out_shape,
        compiler_params=_cparams(("parallel",)),
        name="merge_router" if with_router else "merge",
    )(*args)


def _swiglu_partial(h, wg, wu):
    gate = jnp.dot(h, wg, preferred_element_type=F32)
    up = jnp.dot(h, wu, preferred_element_type=F32)
    return gate * _sigmoid(gate) * up


def _ffn_kernel(h_ref, wg_ref, wu_ref, wd_ref, x_ref, g_ref, gate_ref, o_ref, acc_ref):
    j = pl.program_id(1)

    @pl.when(j == 0)
    def _():
        acc_ref[...] = jnp.zeros_like(acc_ref)

    act = _swiglu_partial(h_ref[...], wg_ref[...], wu_ref[...])
    acc_ref[...] += jnp.dot(act.astype(BF16), wd_ref[...], preferred_element_type=F32)

    @pl.when(j == pl.num_programs(1) - 1)
    def _():
        o_ref[...] = x_ref[...] + gate_ref[...] * _rms(acc_ref[...], g_ref[...])


def _dense_ffn(h, x2d, wg, wu, wd, g, mod, mod_idx, seq):
    tokens = x2d.shape[0]
    tm = min(1024, seq)
    tf = 512
    tpb = seq // tm
    return pl.pallas_call(
        _ffn_kernel,
        grid=(tokens // tm, FFN_DIM // tf),
        in_specs=[
            pl.BlockSpec((tm, D_MODEL), lambda i, j: (i, 0)),
            pl.BlockSpec((D_MODEL, tf), lambda i, j: (0, j)),
            pl.BlockSpec((D_MODEL, tf), lambda i, j: (0, j)),
            pl.BlockSpec((tf, D_MODEL), lambda i, j: (j, 0)),
            pl.BlockSpec((tm, D_MODEL), lambda i, j: (i, 0)),
            pl.BlockSpec((1, D_MODEL), lambda i, j: (0, 0)),
            _mod_spec(mod_idx, 2, tpb),
        ],
        out_specs=pl.BlockSpec((tm, D_MODEL), lambda i, j: (i, 0)),
        out_shape=jax.ShapeDtypeStruct((tokens, D_MODEL), F32),
        scratch_shapes=[pltpu.VMEM((tm, D_MODEL), F32)],
        compiler_params=_cparams(("parallel", "arbitrary")),
        name="dense_ffn",
    )(h, wg, wu, wd, x2d, g, mod)


MOE_TILE = 512
RANK_BLOCK = 512


def _rank_kernel(route_ref, rank_ref, total_ref, count_scr):
    @pl.when(pl.program_id(0) == 0)
    def _():
        count_scr[...] = jnp.zeros_like(count_scr)

    route = route_ref[...]
    lane = lax.broadcasted_iota(jnp.int32, route.shape, 1)
    lane_f = lane.astype(F32)
    e1 = route[:, ROUTE_EXPERT:ROUTE_EXPERT + 1]
    e2 = route[:, ROUTE_EXPERT + 1:ROUTE_EXPERT + 2]
    member = jnp.where(lane_f == e1, 1.0, jnp.where(lane_f == e2, 1.0, 0.0))
    row = lax.broadcasted_iota(jnp.int32, (RANK_BLOCK, RANK_BLOCK), 0)
    col = lax.broadcasted_iota(jnp.int32, (RANK_BLOCK, RANK_BLOCK), 1)
    earlier = jnp.where(col < row, 1.0, 0.0).astype(BF16)
    before = jnp.dot(earlier, member.astype(BF16), preferred_element_type=F32) + count_scr[...]
    rank1 = jnp.sum(jnp.where(lane_f == e1, before, 0.0), axis=1, keepdims=True)
    rank2 = jnp.sum(jnp.where(lane_f == e2, before, 0.0), axis=1, keepdims=True)
    rank_ref[...] = jnp.where(lane == 0, rank1, jnp.where(lane == 1, rank2, 0.0))
    count_scr[...] += jnp.sum(member, axis=0, keepdims=True)
    total_ref[...] = count_scr[...]


def _expert_ranks(route):
    tokens = route.shape[0]
    return pl.pallas_call(
        _rank_kernel,
        grid=(tokens // RANK_BLOCK,),
        in_specs=[pl.BlockSpec((RANK_BLOCK, LANES), lambda i: (i, 0))],
        out_specs=[pl.BlockSpec((RANK_BLOCK, LANES), lambda i: (i, 0)),
                   pl.BlockSpec((1, LANES), lambda i: (0, 0))],
        out_shape=[jax.ShapeDtypeStruct((tokens, LANES), F32),
                   jax.ShapeDtypeStruct((1, LANES), F32)],
        scratch_shapes=[pltpu.VMEM((1, LANES), F32)],
        compiler_params=_cparams(("arbitrary",)),
        name="expert_ranks",
    )(route)


def _routing_tables(route, ranks, totals, n_tiles):
    counts = totals[0, :N_EXPERTS].astype(jnp.int32)
    padded = (counts + MOE_TILE - 1) // MOE_TILE * MOE_TILE
    ends = jnp.cumsum(padded)
    starts = ends - padded
    e1 = route[:, ROUTE_EXPERT].astype(jnp.int32)
    e2 = route[:, ROUTE_EXPERT + 1].astype(jnp.int32)
    pos1 = starts[e1] + ranks[:, 0].astype(jnp.int32)
    pos2 = starts[e2] + ranks[:, 1].astype(jnp.int32)
    n_used = ends[-1:] // MOE_TILE
    tile_start = jnp.minimum(jnp.arange(n_tiles, dtype=jnp.int32), n_used - 1) * MOE_TILE
    tile_expert = jnp.sum(tile_start[:, None] >= ends[None, :], axis=1).astype(jnp.int32)
    return pos1, pos2, tile_expert, n_used.astype(jnp.int32)


def _dispatch_kernel(pos1_ref, pos2_ref, h_ref, init_hbm, xs_hbm, sem, *, tm):
    del init_hbm
    base = pl.program_id(0) * tm

    def row_copy(r, pos_ref):
        return pltpu.make_async_copy(
            h_ref.at[pl.ds(r, 1)], xs_hbm.at[pl.ds(pos_ref[base + r], 1)], sem)

    def issue(r, _):
        row_copy(r, pos1_ref).start()
        row_copy(r, pos2_ref).start()
        return 0

    def drain(r, _):
        row_copy(r, pos1_ref).wait()
        row_copy(r, pos2_ref).wait()
        return 0

    lax.fori_loop(0, tm, issue, 0)
    lax.fori_loop(0, tm, drain, 0)


def _dispatch(h, pos1, pos2, n_rows):
    tokens = h.shape[0]
    tm = 512
    return pl.pallas_call(
        functools.partial(_dispatch_kernel, tm=tm),
        grid_spec=pltpu.PrefetchScalarGridSpec(
            num_scalar_prefetch=2,
            grid=(tokens // tm,),
            in_specs=[pl.BlockSpec((tm, D_MODEL), lambda i, p1, p2: (i, 0)),
                      pl.BlockSpec(memory_space=pl.ANY)],
            out_specs=pl.BlockSpec(memory_space=pl.ANY),
            scratch_shapes=[pltpu.SemaphoreType.DMA],
        ),
        out_shape=jax.ShapeDtypeStruct((n_rows, D_MODEL), F32),
        input_output_aliases={3: 0},
        compiler_params=_cparams(("arbitrary",)),
        name="expert_dispatch",
    )(pos1, pos2, h, jnp.zeros((n_rows, D_MODEL), F32))


def _grouped_ffn_kernel(te_ref, nu_ref, xs_ref, wg_ref, wu_ref, wd_ref, ys_ref, xb_scr, acc_scr):
    del te_ref
    i = pl.program_id(0)
    j = pl.program_id(1)
    last = pl.num_programs(1) - 1
    used = i < nu_ref[0]

    @pl.when(used & (j == 0))
    def _():
        xb_scr[...] = xs_ref[...].astype(BF16)
        acc_scr[...] = jnp.zeros_like(acc_scr)

    @pl.when(used)
    def _():
        act = _swiglu_partial(xb_scr[...], wg_ref[...], wu_ref[...])
        acc_scr[...] += jnp.dot(act.astype(BF16), wd_ref[...], preferred_element_type=F32)

    @pl.when(used & (j == last))
    def _():
        ys_ref[...] = acc_scr[...]

    @pl.when(jnp.logical_not(used) & (j == last))
    def _():
        ys_ref[...] = jnp.zeros_like(ys_ref)


def _grouped_ffn(xs, tile_expert, n_used, wg, wu, wd):
    n_rows = xs.shape[0]
    tf = 512
    nj = FFN_DIM // tf

    def ffn_col(i, j, nu):
        return jnp.where(i < nu[0], j, nj - 1)

    return pl.pallas_call(
        _grouped_ffn_kernel,
        grid_spec=pltpu.PrefetchScalarGridSpec(
            num_scalar_prefetch=2,
            grid=(n_rows // MOE_TILE, nj),
            in_specs=[
                pl.BlockSpec((MOE_TILE, D_MODEL),
                             lambda i, j, te, nu: (jnp.minimum(i, nu[0] - 1), 0)),
                pl.BlockSpec((None, D_MODEL, tf),
                             lambda i, j, te, nu: (te[i], 0, ffn_col(i, j, nu))),
                pl.BlockSpec((None, D_MODEL, tf),
                             lambda i, j, te, nu: (te[i], 0, ffn_col(i, j, nu))),
                pl.BlockSpec((None, tf, D_MODEL),
                             lambda i, j, te, nu: (te[i], ffn_col(i, j, nu), 0)),
            ],
            out_specs=pl.BlockSpec((MOE_TILE, D_MODEL), lambda i, j, te, nu: (i, 0)),
            scratch_shapes=[pltpu.VMEM((MOE_TILE, D_MODEL), BF16),
                            pltpu.VMEM((MOE_TILE, D_MODEL), F32)],
        ),
        out_shape=jax.ShapeDtypeStruct((n_rows, D_MODEL), F32),
        compiler_params=_cparams(("arbitrary", "arbitrary")),
        name="grouped_ffn",
    )(tile_expert, n_used, xs, wg, wu, wd)


def _combine_kernel(pos1_ref, pos2_ref, ys_hbm, route_ref, x_ref, g_ref, gate_ref, o_ref,
                    y1_scr, y2_scr, sem, *, tm):
    base = pl.program_id(0) * tm

    def row_copy(r, pos_ref, dst):
        return pltpu.make_async_copy(
            ys_hbm.at[pl.ds(pos_ref[base + r], 1)], dst.at[pl.ds(r, 1)], sem)

    def issue(r, _):
        row_copy(r, pos1_ref, y1_scr).start()
        row_copy(r, pos2_ref, y2_scr).start()
        return 0

    def drain(r, _):
        row_copy(r, pos1_ref, y1_scr).wait()
        row_copy(r, pos2_ref, y2_scr).wait()
        return 0

    lax.fori_loop(0, tm, issue, 0)
    lax.fori_loop(0, tm, drain, 0)
    route = route_ref[...]
    w1 = route[:, ROUTE_WEIGHT:ROUTE_WEIGHT + 1]
    w2 = route[:, ROUTE_WEIGHT + 1:ROUTE_WEIGHT + 2]
    y = w1 * y1_scr[...] + w2 * y2_scr[...]
    o_ref[...] = x_ref[...] + gate_ref[...] * _rms(y, g_ref[...])


def _combine(ys, pos1, pos2, route, x2d, g, mod, mod_idx, seq):
    tokens = x2d.shape[0]
    tm = min(256, seq)
    tpb = seq // tm
    gate_spec = _mod_spec(mod_idx, 2, tpb)
    return pl.pallas_call(
        functools.partial(_combine_kernel, tm=tm),
        grid_spec=pltpu.PrefetchScalarGridSpec(
            num_scalar_prefetch=2,
            grid=(tokens // tm,),
            in_specs=[
                pl.BlockSpec(memory_space=pl.ANY),
                pl.BlockSpec((tm, LANES), lambda i, p1, p2: (i, 0)),
                pl.BlockSpec((tm, D_MODEL), lambda i, p1, p2: (i, 0)),
                pl.BlockSpec((1, D_MODEL), lambda i, p1, p2: (0, 0)),
                pl.BlockSpec(gate_spec.block_shape, lambda i, p1, p2: gate_spec.index_map(i)),
            ],
            out_specs=pl.BlockSpec((tm, D_MODEL), lambda i, p1, p2: (i, 0)),
            scratch_shapes=[pltpu.VMEM((tm, D_MODEL), F32), pltpu.VMEM((tm, D_MODEL), F32),
                            pltpu.SemaphoreType.DMA],
        ),
        out_shape=jax.ShapeDtypeStruct((tokens, D_MODEL), F32),
        compiler_params=_cparams(("arbitrary",)),
        name="expert_combine",
    )(pos1, pos2, ys, route, x2d, g, mod)


def _moe_ffn(h, route, x2d, wg, wu, wd, g, mod, mod_idx, seq):
    tokens = x2d.shape[0]
    n_rows = 2 * tokens + N_EXPERTS * MOE_TILE
    ranks, totals = _expert_ranks(route)
    pos1, pos2, tile_expert, n_used = _routing_tables(route, ranks, totals, n_rows // MOE_TILE)
    xs = _dispatch(h, pos1, pos2, n_rows)
    ys = _grouped_ffn(xs, tile_expert, n_used, wg, wu, wd)
    return _combine(ys, pos1, pos2, route, x2d, g, mod, mod_idx, seq)


def _prep_w_in(w):
    fw = FOX_HEADS * HEAD_DIM
    sw = SB_HEADS * HEAD_DIM
    dw = DIFF_HEADS * 2 * HEAD_DIM
    sizes = (fw, fw, fw, FOX_HEADS, sw, sw, sw, dw, dw, dw, GATE_COLS)
    offs = [0]
    for s in sizes:
        offs.append(offs[-1] + s)
    part = [w[:, offs[i]:offs[i + 1]] for i in range(len(sizes))]
    fq, fk, fv, ff, sq, sk, sv, dq, dk, dv, gl = part
    w_main = jnp.concatenate([gl, fq, fk, fv, sq, sk, sv, dq, dk, dv], axis=1).astype(BF16)
    w_forget = jnp.pad(ff, ((0, 0), (0, LANES - FOX_HEADS))).astype(BF16)
    return w_main, w_forget


def _pack_router(r):
    p1 = r.astype(BF16)
    r1 = r - p1.astype(F32)
    p2 = r1.astype(BF16)
    p3 = (r1 - p2.astype(F32)).astype(BF16)
    packed = jnp.concatenate([p1, p2, p3], axis=1)
    return jnp.pad(packed, ((0, 0), (0, LANES - 3 * N_EXPERTS)))


def kernel(x, c, w_ada, b_ada, norm_g, w_in, fox_f_bias, diff_lambda, diff_subln_g, w_branch,
           w_out, ffn_w_gate, ffn_w_up, ffn_w_down, moe_router, moe_w_gate, moe_w_up,
           moe_w_down):
    batch, seq, _ = x.shape
    depth = w_in.shape[0]
    tokens = batch * seq
    x2d = x.reshape(tokens, D_MODEL)
    mod = _ada_modulation(c, w_ada, b_ada)
    slopes = jnp.exp2(-ALIBI_MAX_BIAS * jnp.arange(1, DIFF_HEADS + 1, dtype=F32) / DIFF_HEADS)
    slopes = jnp.broadcast_to(slopes[:, None, None], (DIFF_HEADS, 1, LANES))

    for layer in range(depth):
        lam_init = 0.8 - 0.6 * math.exp(-0.3 * layer)
        mod_attn, mod_ffn = 2 * layer, 2 * layer + 1
        w_main, w_forget = _prep_w_in(w_in[layer])
        proj, forget_logits = _in_projection(
            x2d, norm_g[layer, 0][None, :], mod, mod_attn, w_main, w_forget, seq)
        key_bias = _fox_decay(forget_logits, fox_f_bias[layer], batch, seq)

        o_fox = _attention_call(
            _fox_kernel, proj, (COLBLK_FQ, COLBLK_FK, COLBLK_FV), [key_bias],
            [pl.BlockSpec((seq, LANES), lambda b, c: (b, c))],
            [_values_t_scratch(seq), _values_t_scratch(seq)] + _pipeline_scratch(1),
            batch, seq, FOX_HEADS // 2, "fox_attention")
        o_sb = _attention_call(
            _sb_kernel, proj, (COLBLK_SQ, COLBLK_SK, COLBLK_SV), [], [],
            [_values_t_scratch(seq), _values_t_scratch(seq)] + _pipeline_scratch(1),
            batch, seq, SB_HEADS // 2, "stickbreak_attention")
        o_diff = _attention_call(
            functools.partial(_diff_kernel, lam_init=lam_init), proj,
            (COLBLK_DQ, COLBLK_DK, COLBLK_DV),
            [slopes, diff_lambda[layer].astype(F32), diff_subln_g[layer][None, :].astype(F32)],
            [pl.BlockSpec((None, 1, LANES), lambda b, c: (c, 0, 0)),
             pl.BlockSpec((4, HEAD_DIM), lambda b, c: (0, 0)),
             pl.BlockSpec((1, 2 * HEAD_DIM), lambda b, c: (0, 0))],
            [_values_t_scratch(seq), pltpu.VMEM((seq, LANES), BF16)] + _pipeline_scratch(2),
            batch, seq, DIFF_HEADS, "diff_attention")

        is_moe = layer % 2 == 1
        idx = layer // 2
        router_packed = _pack_router(moe_router[idx]) if is_moe else None
        merged = _merge(
            o_fox, o_sb, o_diff, proj, x2d, w_branch[layer].astype(BF16),
            w_out[layer].astype(BF16), norm_g[layer, 1][None, :], norm_g[layer, 2][None, :],
            mod, mod_attn, mod_ffn, router_packed, seq)
        g3 = norm_g[layer, 3][None, :]
        if is_moe:
            x2d, h, combine = merged
            x2d = _moe_ffn(h, combine, x2d, moe_w_gate[idx].astype(BF16),
                           moe_w_up[idx].astype(BF16), moe_w_down[idx].astype(BF16),
                           g3, mod, mod_ffn, seq)
        else:
            x2d, h = merged
            x2d = _dense_ffn(h, x2d, ffn_w_gate[idx].astype(BF16), ffn_w_up[idx].astype(BF16),
                             ffn_w_down[idx].astype(BF16), g3, mod, mod_ffn, seq)
    return x2d.reshape(batch, seq, D_MODEL)
```

```python
import functools
import math

import jax
import jax.numpy as jnp
import numpy as np
from jax import lax
from jax.experimental import pallas as pl
from jax.experimental.pallas import tpu as pltpu

F32 = jnp.float32
BF16 = jnp.bfloat16

D_MODEL = 1024
HEAD_DIM = 64
FOX_HEADS = 8
SB_HEADS = 8
DIFF_HEADS = 4
BRANCH_WIDTH = 512
N_BRANCHES = 3
FFN_DIM = 3584
N_EXPERTS = 8
RMS_EPS = 1e-6
ALIBI_MAX_BIAS = 8.0

LANES = 128
GATE_COLS = N_BRANCHES * D_MODEL
COLBLK_FQ = GATE_COLS // LANES
COLBLK_FK = COLBLK_FQ + 4
COLBLK_FV = COLBLK_FK + 4
COLBLK_SQ = COLBLK_FV + 4
COLBLK_SK = COLBLK_SQ + 4
COLBLK_SV = COLBLK_SK + 4
COLBLK_DQ = COLBLK_SV + 4
COLBLK_DK = COLBLK_DQ + 4
COLBLK_DV = COLBLK_DK + 4
PROJ_COLS = (COLBLK_DV + 4) * LANES

VMEM_LIMIT = 56 * 1024 * 1024
NEG_BIG = -1e30
QK_SCALE = HEAD_DIM ** -0.5


def _cparams(sem):
    return pltpu.CompilerParams(dimension_semantics=sem, vmem_limit_bytes=VMEM_LIMIT)


def _sigmoid(v):
    return 1.0 / (1.0 + jnp.exp(-v))


def _rms(v, g):
    ms = jnp.mean(v * v, axis=-1, keepdims=True)
    return v * lax.rsqrt(ms + RMS_EPS) * g


def _ada_kernel(c_ref, w_ref, b_ref, o_ref):
    c = c_ref[...]
    a = (c * _sigmoid(c)).astype(BF16)
    o_ref[...] = jnp.dot(a, w_ref[...].astype(BF16), preferred_element_type=F32) + b_ref[...]


def _ada_modulation(c, w_ada, b_ada):
    n_mod = w_ada.shape[0] * w_ada.shape[1]
    batch = c.shape[0]
    w = w_ada.reshape(n_mod, D_MODEL, 3 * D_MODEL)
    b = b_ada.reshape(n_mod, 1, 3 * D_MODEL)
    out = pl.pallas_call(
        _ada_kernel,
        grid=(n_mod, 3),
        in_specs=[
            pl.BlockSpec((batch, D_MODEL), lambda m, j: (0, 0)),
            pl.BlockSpec((None, D_MODEL, D_MODEL), lambda m, j: (m, 0, j)),
            pl.BlockSpec((None, 1, D_MODEL), lambda m, j: (m, 0, j)),
        ],
        out_specs=pl.BlockSpec((None, batch, D_MODEL), lambda m, j: (m, 0, j)),
        out_shape=jax.ShapeDtypeStruct((n_mod, batch, 3 * D_MODEL), F32),
        compiler_params=_cparams(("parallel", "parallel")),
        name="ada_modulation",
    )(c, w, b)
    return out.reshape(n_mod, batch, 1, 3 * D_MODEL)


def _mod_spec(mod_idx, part, rows_per_batch_tiles):
    def index(i, *_):
        return (mod_idx, i // rows_per_batch_tiles, 0, part)
    return pl.BlockSpec((None, None, 1, D_MODEL), index)


def _inproj_kernel(x_ref, g_ref, sc_ref, sh_ref, w_ref, wf_ref, o_ref, f_ref, h_scr):
    @pl.when(pl.program_id(1) == 0)
    def _():
        h = _rms(x_ref[...], g_ref[...]) * (1.0 + sc_ref[...]) + sh_ref[...]
        hb = h.astype(BF16)
        h_scr[...] = hb
        f_ref[...] = jnp.dot(hb, wf_ref[...], preferred_element_type=F32)

    o_ref[...] = jnp.dot(h_scr[...], w_ref[...], preferred_element_type=F32).astype(BF16)


def _in_projection(x2d, g, mod, mod_idx, w_main, w_forget, seq):
    tokens = x2d.shape[0]
    tm = min(1024, seq)
    tn = 1536
    tiles_per_batch = seq // tm
    return pl.pallas_call(
        _inproj_kernel,
        grid=(tokens // tm, PROJ_COLS // tn),
        in_specs=[
            pl.BlockSpec((tm, D_MODEL), lambda i, j: (i, 0)),
            pl.BlockSpec((1, D_MODEL), lambda i, j: (0, 0)),
            _mod_spec(mod_idx, 1, tiles_per_batch),
            _mod_spec(mod_idx, 0, tiles_per_batch),
            pl.BlockSpec((D_MODEL, tn), lambda i, j: (0, j)),
            pl.BlockSpec((D_MODEL, LANES), lambda i, j: (0, 0)),
        ],
        out_specs=[
            pl.BlockSpec((tm, tn), lambda i, j: (i, j)),
            pl.BlockSpec((tm, LANES), lambda i, j: (i, 0)),
        ],
        out_shape=[
            jax.ShapeDtypeStruct((tokens, PROJ_COLS), BF16),
            jax.ShapeDtypeStruct((tokens, LANES), F32),
        ],
        scratch_shapes=[pltpu.VMEM((tm, D_MODEL), BF16)],
        compiler_params=_cparams(("parallel", "arbitrary")),
        name="in_projection",
    )(x2d, g, mod, mod, w_main, w_forget)


CUMSUM_BLOCK = 256
BIAS_PIECES = 3


def _split3(v):
    p1 = v.astype(BF16).astype(F32)
    r1 = v - p1
    p2 = r1.astype(BF16).astype(F32)
    p3 = (r1 - p2).astype(BF16).astype(F32)
    return p1, p2, p3


def _decay_kernel(f_ref, b_ref, sel_ref, o_ref):
    seq = f_ref.shape[0]
    row = lax.broadcasted_iota(jnp.int32, (CUMSUM_BLOCK, CUMSUM_BLOCK), 0)
    col = lax.broadcasted_iota(jnp.int32, (CUMSUM_BLOCK, CUMSUM_BLOCK), 1)
    tri = jnp.where(col <= row, 1.0, 0.0).astype(BF16)
    carry = jnp.zeros((1, LANES), F32)
    for blk in range(seq // CUMSUM_BLOCK):
        rows = slice(blk * CUMSUM_BLOCK, (blk + 1) * CUMSUM_BLOCK)
        z = f_ref[rows, :] + b_ref[...]
        logf = jnp.minimum(z, 0.0) - jnp.log1p(jnp.exp(-jnp.abs(z)))
        cum = carry
        for piece in _split3(logf):
            cum = cum + jnp.dot(tri, piece.astype(BF16), preferred_element_type=F32)
        carry = cum[CUMSUM_BLOCK - 1:CUMSUM_BLOCK, :]
        bias = None
        for i, piece in enumerate(_split3(-cum)):
            part = jnp.dot(piece.astype(BF16), sel_ref[i], preferred_element_type=F32)
            bias = part if bias is None else bias + part
        o_ref[rows, :] = bias.astype(BF16)


def _decay_selectors():
    sel = np.zeros((BIAS_PIECES, LANES, FOX_HEADS // 2 * LANES), np.float32)
    for h in range(FOX_HEADS):
        for i in range(BIAS_PIECES):
            sel[i, h, (h // 2) * LANES + (h % 2) * BIAS_PIECES + i] = 1.0
    return jnp.asarray(sel, BF16)


def _fox_decay(forget_logits, bias, batch, seq):
    bias_row = jnp.zeros((1, LANES), F32).at[0, :FOX_HEADS].set(bias.astype(F32))
    width = FOX_HEADS // 2 * LANES
    return pl.pallas_call(
        _decay_kernel,
        grid=(batch,),
        in_specs=[
            pl.BlockSpec((seq, LANES), lambda b: (b, 0)),
            pl.BlockSpec((1, LANES), lambda b: (0, 0)),
            pl.BlockSpec((BIAS_PIECES, LANES, width), lambda b: (0, 0, 0)),
        ],
        out_specs=pl.BlockSpec((seq, width), lambda b: (b, 0)),
        out_shape=jax.ShapeDtypeStruct((batch * seq, width), BF16),
        compiler_params=_cparams(("parallel",)),
        name="fox_decay",
    )(forget_logits, bias_row, _decay_selectors())


ATT_BLOCK = 256


def _transposed_queries(q):
    return q.astype(F32).T.astype(BF16)


def _head_halves(x, axis):
    low = lax.broadcasted_iota(jnp.int32, x.shape, axis) < HEAD_DIM
    zero = jnp.zeros_like(x)
    return jnp.where(low, x, zero), jnp.where(low, zero, x)


def _key_query_masks(t):
    key = lax.broadcasted_iota(jnp.int32, (t, t), 0)
    query = lax.broadcasted_iota(jnp.int32, (t, t), 1)
    return key <= query, key < query


def _bias_ones(first_lane):
    lane = lax.broadcasted_iota(jnp.int32, (ATT_BLOCK, LANES), 1)
    hit = (lane >= first_lane) & (lane < first_lane + BIAS_PIECES)
    return jnp.where(hit, 1.0, 0.0).astype(BF16)


def _store_values_t(v_ref, vt_refs):
    t = ATT_BLOCK
    for blk in range(v_ref.shape[0] // t):
        vt = v_ref[blk * t:(blk + 1) * t, :].astype(F32).T
        if len(vt_refs) == 2:
            va, vb = _head_halves(vt, 0)
            vt_refs[0][blk] = va.astype(BF16)
            vt_refs[1][blk] = vb.astype(BF16)
        else:
            vt_refs[0][blk] = vt.astype(BF16)


def _softmax_step_t(s, m, l):
    m_new = jnp.maximum(m, jnp.max(s, axis=0, keepdims=True))
    alpha = jnp.exp(m - m_new)
    p = jnp.exp(s - m_new)
    l = alpha * l + jnp.sum(p, axis=0, keepdims=True)
    return m_new, l, alpha, p.astype(BF16)


def _stat_init(t):
    return jnp.full((1, t), NEG_BIG, F32), jnp.zeros((1, t), F32)


def _fox_kernel(q_ref, k_ref, v_ref, kb_ref, o_ref, vta_ref, vtb_ref, s_scr, p_scr, acc_scr):
    t = ATT_BLOCK
    n_blocks = q_ref.shape[0] // t
    _store_values_t(v_ref, (vta_ref, vtb_ref))
    causal, _ = _key_query_masks(t)
    ones_a, ones_b = _bias_ones(0), _bias_ones(BIAS_PIECES)
    head_a_rows = lax.broadcasted_iota(jnp.int32, (LANES, t), 0) < HEAD_DIM

    def q_block(qi, _):
        q0 = pl.multiple_of(qi * t, t)
        qa, qb = _head_halves(q_ref[pl.ds(q0, t), :] * QK_SCALE, 1)
        qta = _transposed_queries(jnp.concatenate([qa, ones_a], axis=1))
        qtb = _transposed_queries(jnp.concatenate([qb, ones_b], axis=1))

        def scores(j):
            k0 = pl.multiple_of(j * t, t)
            k = jnp.concatenate([k_ref[pl.ds(k0, t), :], kb_ref[pl.ds(k0, t), :]], axis=1)
            return (jnp.dot(k, qta, preferred_element_type=F32),
                    jnp.dot(k, qtb, preferred_element_type=F32))

        def accumulate(j, pa, pb, alpha_a, alpha_b):
            acc_scr[...] = (jnp.where(head_a_rows, alpha_a, alpha_b) * acc_scr[...]
                            + jnp.dot(vta_ref[j], pa, preferred_element_type=F32)
                            + jnp.dot(vtb_ref[j], pb, preferred_element_type=F32))

        def stage(s_next, pa, pb):
            s_scr[0], s_scr[1] = s_next
            p_scr[0], p_scr[1] = pa, pb

        def trip(j, c):
            alpha_a, alpha_b, ma, la, mb, lb = c
            s_next = scores(j + 1)
            accumulate(jnp.maximum(j - 1, 0), p_scr[0], p_scr[1], alpha_a, alpha_b)
            ma, la, alpha_a, pa = _softmax_step_t(s_scr[0], ma, la)
            mb, lb, alpha_b, pb = _softmax_step_t(s_scr[1], mb, lb)
            stage(s_next, pa, pb)
            return alpha_a, alpha_b, ma, la, mb, lb

        no_p = jnp.zeros((t, t), BF16)
        one_row = jnp.ones((1, t), F32)
        stage(scores(0), no_p, no_p)
        acc_scr[...] = jnp.zeros_like(acc_scr)
        alpha_a, alpha_b, ma, la, mb, lb = lax.fori_loop(
            0, qi, trip, (one_row, one_row) + _stat_init(t) + _stat_init(t))
        accumulate(jnp.maximum(qi - 1, 0), p_scr[0], p_scr[1], alpha_a, alpha_b)
        _, la, alpha_a, pa = _softmax_step_t(jnp.where(causal, s_scr[0], NEG_BIG), ma, la)
        _, lb, alpha_b, pb = _softmax_step_t(jnp.where(causal, s_scr[1], NEG_BIG), mb, lb)
        accumulate(qi, pa, pb, alpha_a, alpha_b)
        o_t = acc_scr[...] / jnp.where(head_a_rows, la, lb)
        o_ref[pl.ds(q0, t), :] = o_t.T.astype(o_ref.dtype)
        return 0

    lax.fori_loop(0, n_blocks, q_block, 0)


def _sb_kernel(q_ref, k_ref, v_ref, o_ref, vta_ref, vtb_ref, z_scr, a_scr, acc_scr):
    t = ATT_BLOCK
    n_blocks = q_ref.shape[0] // t
    _store_values_t(v_ref, (vta_ref, vtb_ref))
    _, strict = _key_query_masks(t)
    row = lax.broadcasted_iota(jnp.int32, (t, t), 0)
    col = lax.broadcasted_iota(jnp.int32, (t, t), 1)
    later_keys = jnp.where(col > row, 1.0, 0.0).astype(BF16)

    def q_block(qi, _):
        q0 = pl.multiple_of(qi * t, t)
        qa, qb = _head_halves(q_ref[pl.ds(q0, t), :] * QK_SCALE, 1)
        qta, qtb = _transposed_queries(qa), _transposed_queries(qb)

        def scores(step):
            j = jnp.maximum(qi - step, 0)
            k = k_ref[pl.ds(pl.multiple_of(j * t, t), t), :]
            return (jnp.dot(k, qta, preferred_element_type=F32),
                    jnp.dot(k, qtb, preferred_element_type=F32))

        def weights(z, later, diagonal):
            softplus = jnp.maximum(z, 0.0) + jnp.log(1.0 + jnp.exp(-jnp.abs(z)))
            log_stay = -softplus
            if diagonal:
                log_stay = jnp.where(strict, log_stay, 0.0)
            tail = jnp.dot(later_keys, log_stay.astype(BF16),
                           preferred_element_type=F32) + later
            a = jnp.exp(z + log_stay + tail)
            if diagonal:
                a = jnp.where(strict, a, 0.0)
            return a.astype(BF16), later + jnp.sum(log_stay, axis=0, keepdims=True)

        def accumulate(step, aa, ab):
            j = qi - step
            acc_scr[...] += (jnp.dot(vta_ref[j], aa, preferred_element_type=F32)
                             + jnp.dot(vtb_ref[j], ab, preferred_element_type=F32))

        def stage(z_next, aa, ab):
            z_scr[0], z_scr[1] = z_next
            a_scr[0], a_scr[1] = aa, ab

        def trip(step, c):
            later_a, later_b = c
            z_next = scores(step + 1)
            accumulate(step - 1, a_scr[0], a_scr[1])
            aa, later_a = weights(z_scr[0], later_a, False)
            ab, later_b = weights(z_scr[1], later_b, False)
            stage(z_next, aa, ab)
            return later_a, later_b

        za, zb = scores(0)
        z_next = scores(1)
        zero_row = jnp.zeros((1, t), F32)
        aa, later_a = weights(za, zero_row, True)
        ab, later_b = weights(zb, zero_row, True)
        stage(z_next, aa, ab)
        acc_scr[...] = jnp.zeros_like(acc_scr)
        lax.fori_loop(1, qi + 1, trip, (later_a, later_b))
        accumulate(qi, a_scr[0], a_scr[1])
        o_ref[pl.ds(q0, t), :] = acc_scr[...].T.astype(o_ref.dtype)
        return 0

    lax.fori_loop(0, n_blocks, q_block, 0)


def _diff_kernel(q_ref, k_ref, v_ref, slope_ref, dl_ref, g_ref, o_ref, vt_ref, kb_ref,
                 s_scr, p_scr, acc_scr, *, lam_init):
    t = ATT_BLOCK
    n_blocks = q_ref.shape[0] // t
    _store_values_t(v_ref, (vt_ref,))
    slope = slope_ref[:, 0:1]
    lane = lax.broadcasted_iota(jnp.int32, (t, LANES), 1)
    for blk in range(n_blocks):
        pos = (lax.broadcasted_iota(jnp.int32, (t, LANES), 0) + blk * t).astype(F32)
        p1, p2, p3 = _split3(slope * pos)
        bias = jnp.where(lane == 0, p1, jnp.where(lane == 1, p2, jnp.where(lane == 2, p3, 0.0)))
        kb_ref[blk * t:(blk + 1) * t, :] = bias.astype(BF16)
    causal, _ = _key_query_masks(t)
    ones = _bias_ones(0)
    dl = dl_ref[...]
    lam = (jnp.exp(jnp.sum(dl[0:1] * dl[1:2], axis=1, keepdims=True))
           - jnp.exp(jnp.sum(dl[2:3] * dl[3:4], axis=1, keepdims=True)) + lam_init)

    def q_block(qi, _):
        q0 = pl.multiple_of(qi * t, t)
        q1, q2 = _head_halves(q_ref[pl.ds(q0, t), :] * QK_SCALE, 1)
        qt1 = _transposed_queries(jnp.concatenate([q1, ones], axis=1))
        qt2 = _transposed_queries(jnp.concatenate([q2, ones], axis=1))

        def scores(j):
            k0 = pl.multiple_of(j * t, t)
            k = jnp.concatenate([k_ref[pl.ds(k0, t), :], kb_ref[pl.ds(k0, t), :]], axis=1)
            return (jnp.dot(k, qt1, preferred_element_type=F32),
                    jnp.dot(k, qt2, preferred_element_type=F32))

        def accumulate(j, p1, p2, alpha1, alpha2):
            vt = vt_ref[j]
            acc_scr[0] = alpha1 * acc_scr[0] + jnp.dot(vt, p1, preferred_element_type=F32)
            acc_scr[1] = alpha2 * acc_scr[1] + jnp.dot(vt, p2, preferred_element_type=F32)

        def stage(s_next, p1, p2):
            s_scr[0], s_scr[1] = s_next
            p_scr[0], p_scr[1] = p1, p2

        def trip(j, c):
            alpha1, alpha2, m1, l1, m2, l2 = c
            s_next = scores(j + 1)
            accumulate(jnp.maximum(j - 1, 0), p_scr[0], p_scr[1], alpha1, alpha2)
            m1, l1, alpha1, p1 = _softmax_step_t(s_scr[0], m1, l1)
            m2, l2, alpha2, p2 = _softmax_step_t(s_scr[1], m2, l2)
            stage(s_next, p1, p2)
            return alpha1, alpha2, m1, l1, m2, l2

        no_p = jnp.zeros((t, t), BF16)
        one_row = jnp.ones((1, t), F32)
        stage(scores(0), no_p, no_p)
        acc_scr[...] = jnp.zeros_like(acc_scr)
        alpha1, alpha2, m1, l1, m2, l2 = lax.fori_loop(
            0, qi, trip, (one_row, one_row) + _stat_init(t) + _stat_init(t))
        accumulate(jnp.maximum(qi - 1, 0), p_scr[0], p_scr[1], alpha1, alpha2)
        _, l1, alpha1, p1 = _softmax_step_t(jnp.where(causal, s_scr[0], NEG_BIG), m1, l1)
        _, l2, alpha2, p2 = _softmax_step_t(jnp.where(causal, s_scr[1], NEG_BIG), m2, l2)
        accumulate(qi, p1, p2, alpha1, alpha2)
        o = (acc_scr[0] / l1 - lam * (acc_scr[1] / l2)).T
        o_ref[pl.ds(q0, t), :] = (_rms(o, g_ref[...]) * (1.0 - lam_init)).astype(o_ref.dtype)
        return 0

    lax.fori_loop(0, n_blocks, q_block, 0)


def _attention_call(kernel, proj, colblks, extra_in, extra_specs, scratch, batch, seq,
                    n_colblk, name):
    qc, kc, vc = colblks
    return pl.pallas_call(
        kernel,
        grid=(batch, n_colblk),
        in_specs=[
            pl.BlockSpec((seq, LANES), lambda b, c: (b, qc + c)),
            pl.BlockSpec((seq, LANES), lambda b, c: (b, kc + c)),
            pl.BlockSpec((seq, LANES), lambda b, c: (b, vc + c)),
        ] + extra_specs,
        out_specs=pl.BlockSpec((seq, LANES), lambda b, c: (b, c)),
        out_shape=jax.ShapeDtypeStruct((batch * seq, n_colblk * LANES), BF16),
        scratch_shapes=scratch,
        compiler_params=_cparams(("parallel", "parallel")),
        name=name,
    )(proj, proj, proj, *extra_in)


def _values_t_scratch(seq):
    return pltpu.VMEM((seq // ATT_BLOCK, LANES, ATT_BLOCK), BF16)


def _pipeline_scratch(n_acc):
    acc_shape = (LANES, ATT_BLOCK) if n_acc == 1 else (n_acc, LANES, ATT_BLOCK)
    return [pltpu.VMEM((2, ATT_BLOCK, ATT_BLOCK), F32),
            pltpu.VMEM((2, ATT_BLOCK, ATT_BLOCK), BF16),
            pltpu.VMEM(acc_shape, F32)]


ROUTE_EXPERT = 0
ROUTE_WEIGHT = 2


def _route(h, r_ref):
    h1 = h.astype(BF16)
    h2 = (h - h1.astype(F32)).astype(BF16)
    prod = (jnp.dot(h1, r_ref[...], preferred_element_type=F32)
            + jnp.dot(h2, r_ref[...], preferred_element_type=F32))
    lane = lax.broadcasted_iota(jnp.int32, prod.shape, 1)
    shifted1 = pltpu.roll(prod, LANES - N_EXPERTS, axis=1)
    shifted2 = pltpu.roll(prod, LANES - 2 * N_EXPERTS, axis=1)
    logits = jnp.where(lane < N_EXPERTS, prod + shifted1 + shifted2, NEG_BIG)
    v1 = jnp.max(logits, axis=1, keepdims=True)
    i1 = jnp.min(jnp.where(logits == v1, lane, LANES), axis=1, keepdims=True)
    rest = jnp.where(lane == i1, NEG_BIG, logits)
    v2 = jnp.max(rest, axis=1, keepdims=True)
    i2 = jnp.min(jnp.where(rest == v2, lane, LANES), axis=1, keepdims=True)
    e2 = jnp.exp(v2 - v1)
    w1 = 1.0 / (1.0 + e2)
    w2 = e2 / (1.0 + e2)
    record = jnp.where(lane == ROUTE_EXPERT, i1.astype(F32), 0.0)
    record = jnp.where(lane == ROUTE_EXPERT + 1, i2.astype(F32), record)
    record = jnp.where(lane == ROUTE_WEIGHT, w1, record)
    return jnp.where(lane == ROUTE_WEIGHT + 1, w2, record)


def _merge_kernel(*refs, with_router):
    if with_router:
        (of_ref, os_ref, od_ref, gl_ref, x_ref, wb_ref, wo_ref, g1_ref, gate_ref,
         g2_ref, sc_ref, sh_ref, r_ref, xo_ref, ho_ref, cw_ref) = refs
    else:
        (of_ref, os_ref, od_ref, gl_ref, x_ref, wb_ref, wo_ref, g1_ref, gate_ref,
         g2_ref, sc_ref, sh_ref, xo_ref, ho_ref) = refs
    merged = None
    for n, o_ref in enumerate((of_ref, os_ref, od_ref)):
        p = jnp.dot(o_ref[...], wb_ref[n], preferred_element_type=F32)
        gated = _sigmoid(gl_ref[:, n * D_MODEL:(n + 1) * D_MODEL].astype(F32)) * p
        merged = gated if merged is None else merged + gated
    y = jnp.dot(merged.astype(BF16), wo_ref[...], preferred_element_type=F32)
    xn = x_ref[...] + gate_ref[...] * _rms(y, g1_ref[...])
    xo_ref[...] = xn
    h = _rms(xn, g2_ref[...]) * (1.0 + sc_ref[...]) + sh_ref[...]
    ho_ref[...] = h.astype(ho_ref.dtype)
    if with_router:
        cw_ref[...] = _route(h, r_ref)


def _merge(o_fox, o_sb, o_diff, proj, x2d, w_branch, w_out, g1, g2, mod, mod_attn, mod_ffn,
           router_packed, seq):
    tokens = x2d.shape[0]
    tm = min(512, seq)
    tpb = seq // tm
    with_router = router_packed is not None
    row = lambda i: (i, 0)
    const2 = lambda i: (0, 0)
    in_specs = [
        pl.BlockSpec((tm, BRANCH_WIDTH), row),
        pl.BlockSpec((tm, BRANCH_WIDTH), row),
        pl.BlockSpec((tm, BRANCH_WIDTH), row),
        pl.BlockSpec((tm, GATE_COLS), row),
        pl.BlockSpec((tm, D_MODEL), row),
        pl.BlockSpec((N_BRANCHES, BRANCH_WIDTH, D_MODEL), lambda i: (0, 0, 0)),
        pl.BlockSpec((D_MODEL, D_MODEL), const2),
        pl.BlockSpec((1, D_MODEL), const2),
        _mod_spec(mod_attn, 2, tpb),
        pl.BlockSpec((1, D_MODEL), const2),
        _mod_spec(mod_ffn, 1, tpb),
        _mod_spec(mod_ffn, 0, tpb),
    ]
    args = [o_fox, o_sb, o_diff, proj, x2d, w_branch, w_out, g1, mod, g2, mod, mod]
    out_specs = [pl.BlockSpec((tm, D_MODEL), row), pl.BlockSpec((tm, D_MODEL), row)]
    out_shape = [jax.ShapeDtypeStruct((tokens, D_MODEL), F32),
                 jax.ShapeDtypeStruct((tokens, D_MODEL), F32 if with_router else BF16)]
    if with_router:
        in_specs.append(pl.BlockSpec((D_MODEL, LANES), const2))
        args.append(router_packed)
        out_specs.append(pl.BlockSpec((tm, LANES), row))
        out_shape.append(jax.ShapeDtypeStruct((tokens, LANES), F32))
    return pl.pallas_call(
        functools.partial(_merge_kernel, with_router=with_router),
        grid=(tokens // tm,),
        in_specs=in_specs,
        out_specs=out_specs,
        out_shape=out_shape,
        compiler_params=_cparams(("parallel",)),
        name="merge_router" if with_router else "merge",
    )(*args)


def _swiglu_partial(h, wg, wu):
    gate = jnp.dot(h, wg, preferred_element_type=F32)
    up = jnp.dot(h, wu, preferred_element_type=F32)
    return gate * _sigmoid(gate) * up


def _ffn_kernel(h_ref, wg_ref, wu_ref, wd_ref, x_ref, g_ref, gate_ref, o_ref, acc_ref):
    j = pl.program_id(1)

    @pl.when(j == 0)
    def _():
        acc_ref[...] = jnp.zeros_like(acc_ref)

    act = _swiglu_partial(h_ref[...], wg_ref[...], wu_ref[...])
    acc_ref[...] += jnp.dot(act.astype(BF16), wd_ref[...], preferred_element_type=F32)

    @pl.when(j == pl.num_programs(1) - 1)
    def _():
        o_ref[...] = x_ref[...] + gate_ref[...] * _rms(acc_ref[...], g_ref[...])


def _dense_ffn(h, x2d, wg, wu, wd, g, mod, mod_idx, seq):
    tokens = x2d.shape[0]
    tm = min(1024, seq)
    tf = 512
    tpb = seq // tm
    return pl.pallas_call(
        _ffn_kernel,
        grid=(tokens // tm, FFN_DIM // tf),
        in_specs=[
            pl.BlockSpec((tm, D_MODEL), lambda i, j: (i, 0)),
            pl.BlockSpec((D_MODEL, tf), lambda i, j: (0, j)),
            pl.BlockSpec((D_MODEL, tf), lambda i, j: (0, j)),
            pl.BlockSpec((tf, D_MODEL), lambda i, j: (j, 0)),
            pl.BlockSpec((tm, D_MODEL), lambda i, j: (i, 0)),
            pl.BlockSpec((1, D_MODEL), lambda i, j: (0, 0)),
            _mod_spec(mod_idx, 2, tpb),
        ],
        out_specs=pl.BlockSpec((tm, D_MODEL), lambda i, j: (i, 0)),
        out_shape=jax.ShapeDtypeStruct((tokens, D_MODEL), F32),
        scratch_shapes=[pltpu.VMEM((tm, D_MODEL), F32)],
        compiler_params=_cparams(("parallel", "arbitrary")),
        name="dense_ffn",
    )(h, wg, wu, wd, x2d, g, mod)


MOE_TILE = 512
RANK_BLOCK = 512
ROW_DMA_UNROLL = 8


def _rank_kernel(route_ref, rank_ref, total_ref, count_scr):
    @pl.when(pl.program_id(0) == 0)
    def _():
        count_scr[...] = jnp.zeros_like(count_scr)

    route = route_ref[...]
    lane = lax.broadcasted_iota(jnp.int32, route.shape, 1)
    lane_f = lane.astype(F32)
    e1 = route[:, ROUTE_EXPERT:ROUTE_EXPERT + 1]
    e2 = route[:, ROUTE_EXPERT + 1:ROUTE_EXPERT + 2]
    member = jnp.where(lane_f == e1, 1.0, jnp.where(lane_f == e2, 1.0, 0.0))
    row = lax.broadcasted_iota(jnp.int32, (RANK_BLOCK, RANK_BLOCK), 0)
    col = lax.broadcasted_iota(jnp.int32, (RANK_BLOCK, RANK_BLOCK), 1)
    earlier = jnp.where(col < row, 1.0, 0.0).astype(BF16)
    before = jnp.dot(earlier, member.astype(BF16), preferred_element_type=F32) + count_scr[...]
    rank1 = jnp.sum(jnp.where(lane_f == e1, before, 0.0), axis=1, keepdims=True)
    rank2 = jnp.sum(jnp.where(lane_f == e2, before, 0.0), axis=1, keepdims=True)
    rank_ref[...] = jnp.where(lane == 0, rank1, jnp.where(lane == 1, rank2, 0.0))
    count_scr[...] += jnp.sum(member, axis=0, keepdims=True)
    total_ref[...] = count_scr[...]


def _expert_ranks(route):
    tokens = route.shape[0]
    return pl.pallas_call(
        _rank_kernel,
        grid=(tokens // RANK_BLOCK,),
        in_specs=[pl.BlockSpec((RANK_BLOCK, LANES), lambda i: (i, 0))],
        out_specs=[pl.BlockSpec((RANK_BLOCK, LANES), lambda i: (i, 0)),
                   pl.BlockSpec((1, LANES), lambda i: (0, 0))],
        out_shape=[jax.ShapeDtypeStruct((tokens, LANES), F32),
                   jax.ShapeDtypeStruct((1, LANES), F32)],
        scratch_shapes=[pltpu.VMEM((1, LANES), F32)],
        compiler_params=_cparams(("arbitrary",)),
        name="expert_ranks",
    )(route)


def _routing_tables(route, ranks, totals, n_tiles):
    counts = totals[0, :N_EXPERTS].astype(jnp.int32)
    padded = (counts + MOE_TILE - 1) // MOE_TILE * MOE_TILE
    ends = jnp.cumsum(padded)
    starts = ends - padded
    e1 = route[:, ROUTE_EXPERT].astype(jnp.int32)
    e2 = route[:, ROUTE_EXPERT + 1].astype(jnp.int32)
    pos1 = starts[e1] + ranks[:, 0].astype(jnp.int32)
    pos2 = starts[e2] + ranks[:, 1].astype(jnp.int32)
    n_used = ends[-1:] // MOE_TILE
    tile_start = jnp.minimum(jnp.arange(n_tiles, dtype=jnp.int32), n_used - 1) * MOE_TILE
    tile_expert = jnp.sum(tile_start[:, None] >= ends[None, :], axis=1).astype(jnp.int32)
    return pos1, pos2, tile_expert, n_used.astype(jnp.int32)


def _dispatch_kernel(pos1_ref, pos2_ref, h_ref, init_hbm, xs_hbm, sem, *, tm):
    del init_hbm
    base = pl.program_id(0) * tm

    def row_copy(r, pos_ref):
        return pltpu.make_async_copy(
            h_ref.at[pl.ds(r, 1)], xs_hbm.at[pl.ds(pos_ref[base + r], 1)], sem)

    def issue(r, _):
        row_copy(r, pos1_ref).start()
        row_copy(r, pos2_ref).start()
        return 0

    def drain(r, _):
        row_copy(r, pos1_ref).wait()
        row_copy(r, pos2_ref).wait()
        return 0

    lax.fori_loop(0, tm, issue, 0, unroll=ROW_DMA_UNROLL)
    lax.fori_loop(0, tm, drain, 0, unroll=ROW_DMA_UNROLL)


def _dispatch(h, pos1, pos2, n_rows):
    tokens = h.shape[0]
    tm = 512
    return pl.pallas_call(
        functools.partial(_dispatch_kernel, tm=tm),
        grid_spec=pltpu.PrefetchScalarGridSpec(
            num_scalar_prefetch=2,
            grid=(tokens // tm,),
            in_specs=[pl.BlockSpec((tm, D_MODEL), lambda i, p1, p2: (i, 0)),
                      pl.BlockSpec(memory_space=pl.ANY)],
            out_specs=pl.BlockSpec(memory_space=pl.ANY),
            scratch_shapes=[pltpu.SemaphoreType.DMA],
        ),
        out_shape=jax.ShapeDtypeStruct((n_rows, D_MODEL), F32),
        input_output_aliases={3: 0},
        compiler_params=_cparams(("arbitrary",)),
        name="expert_dispatch",
    )(pos1, pos2, h, jnp.zeros((n_rows, D_MODEL), F32))


def _grouped_ffn_kernel(te_ref, nu_ref, xs_ref, wg_ref, wu_ref, wd_ref, ys_ref, xb_scr, acc_scr):
    del te_ref
    i = pl.program_id(0)
    j = pl.program_id(1)
    last = pl.num_programs(1) - 1
    used = i < nu_ref[0]

    @pl.when(used & (j == 0))
    def _():
        xb_scr[...] = xs_ref[...].astype(BF16)
        acc_scr[...] = jnp.zeros_like(acc_scr)

    @pl.when(used)
    def _():
        act = _swiglu_partial(xb_scr[...], wg_ref[...], wu_ref[...])
        acc_scr[...] += jnp.dot(act.astype(BF16), wd_ref[...], preferred_element_type=F32)

    @pl.when(used & (j == last))
    def _():
        ys_ref[...] = acc_scr[...]

    @pl.when(jnp.logical_not(used) & (j == last))
    def _():
        ys_ref[...] = jnp.zeros_like(ys_ref)


def _grouped_ffn(xs, tile_expert, n_used, wg, wu, wd):
    n_rows = xs.shape[0]
    tf = 512
    nj = FFN_DIM // tf

    def ffn_col(i, j, nu):
        return jnp.where(i < nu[0], j, nj - 1)

    return pl.pallas_call(
        _grouped_ffn_kernel,
        grid_spec=pltpu.PrefetchScalarGridSpec(
            num_scalar_prefetch=2,
            grid=(n_rows // MOE_TILE, nj),
            in_specs=[
                pl.BlockSpec((MOE_TILE, D_MODEL),
                             lambda i, j, te, nu: (jnp.minimum(i, nu[0] - 1), 0)),
                pl.BlockSpec((None, D_MODEL, tf),
                             lambda i, j, te, nu: (te[i], 0, ffn_col(i, j, nu))),
                pl.BlockSpec((None, D_MODEL, tf),
                             lambda i, j, te, nu: (te[i], 0, ffn_col(i, j, nu))),
                pl.BlockSpec((None, tf, D_MODEL),
                             lambda i, j, te, nu: (te[i], ffn_col(i, j, nu), 0)),
            ],
            out_specs=pl.BlockSpec((MOE_TILE, D_MODEL), lambda i, j, te, nu: (i, 0)),
            scratch_shapes=[pltpu.VMEM((MOE_TILE, D_MODEL), BF16),
                            pltpu.VMEM((MOE_TILE, D_MODEL), F32)],
        ),
        out_shape=jax.ShapeDtypeStruct((n_rows, D_MODEL), F32),
        compiler_params=_cparams(("arbitrary", "arbitrary")),
        name="grouped_ffn",
    )(tile_expert, n_used, xs, wg, wu, wd)


def _combine_kernel(pos1_ref, pos2_ref, ys_hbm, route_ref, x_ref, g_ref, gate_ref, o_ref,
                    y1_scr, y2_scr, sem, *, tm):
    base = pl.program_id(0) * tm

    def row_copy(r, pos_ref, dst):
        return pltpu.make_async_copy(
            ys_hbm.at[pl.ds(pos_ref[base + r], 1)], dst.at[pl.ds(r, 1)], sem)

    def issue(r, _):
        row_copy(r, pos1_ref, y1_scr).start()
        row_copy(r, pos2_ref, y2_scr).start()
        return 0

    def drain(r, _):
        row_copy(r, pos1_ref, y1_scr).wait()
        row_copy(r, pos2_ref, y2_scr).wait()
        return 0

    lax.fori_loop(0, tm, issue, 0, unroll=ROW_DMA_UNROLL)
    lax.fori_loop(0, tm, drain, 0, unroll=ROW_DMA_UNROLL)
    route = route_ref[...]
    w1 = route[:, ROUTE_WEIGHT:ROUTE_WEIGHT + 1]
    w2 = route[:, ROUTE_WEIGHT + 1:ROUTE_WEIGHT + 2]
    y = w1 * y1_scr[...] + w2 * y2_scr[...]
    o_ref[...] = x_ref[...] + gate_ref[...] * _rms(y, g_ref[...])


def _combine(ys, pos1, pos2, route, x2d, g, mod, mod_idx, seq):
    tokens = x2d.shape[0]
    tm = min(256, seq)
    tpb = seq // tm
    gate_spec = _mod_spec(mod_idx, 2, tpb)
    return pl.pallas_call(
        functools.partial(_combine_kernel, tm=tm),
        grid_spec=pltpu.PrefetchScalarGridSpec(
            num_scalar_prefetch=2,
            grid=(tokens // tm,),
            in_specs=[
                pl.BlockSpec(memory_space=pl.ANY),
                pl.BlockSpec((tm, LANES), lambda i, p1, p2: (i, 0)),
                pl.BlockSpec((tm, D_MODEL), lambda i, p1, p2: (i, 0)),
                pl.BlockSpec((1, D_MODEL), lambda i, p1, p2: (0, 0)),
                pl.BlockSpec(gate_spec.block_shape, lambda i, p1, p2: gate_spec.index_map(i)),
            ],
            out_specs=pl.BlockSpec((tm, D_MODEL), lambda i, p1, p2: (i, 0)),
            scratch_shapes=[pltpu.VMEM((tm, D_MODEL), F32), pltpu.VMEM((tm, D_MODEL), F32),
                            pltpu.SemaphoreType.DMA],
        ),
        out_shape=jax.ShapeDtypeStruct((tokens, D_MODEL), F32),
        compiler_params=_cparams(("arbitrary",)),
        name="expert_combine",
    )(pos1, pos2, ys, route, x2d, g, mod)


def _moe_ffn(h, route, x2d, wg, wu, wd, g, mod, mod_idx, seq):
    tokens = x2d.shape[0]
    n_rows = 2 * tokens + N_EXPERTS * MOE_TILE
    ranks, totals = _expert_ranks(route)
    pos1, pos2, tile_expert, n_used = _routing_tables(route, ranks, totals, n_rows // MOE_TILE)
    xs = _dispatch(h, pos1, pos2, n_rows)
    ys = _grouped_ffn(xs, tile_expert, n_used, wg, wu, wd)
    return _combine(ys, pos1, pos2, route, x2d, g, mod, mod_idx, seq)


def _prep_w_in(w):
    fw = FOX_HEADS * HEAD_DIM
    sw = SB_HEADS * HEAD_DIM
    dw = DIFF_HEADS * 2 * HEAD_DIM
    sizes = (fw, fw, fw, FOX_HEADS, sw, sw, sw, dw, dw, dw, GATE_COLS)
    offs = [0]
    for s in sizes:
        offs.append(offs[-1] + s)
    part = [w[:, offs[i]:offs[i + 1]] for i in range(len(sizes))]
    fq, fk, fv, ff, sq, sk, sv, dq, dk, dv, gl = part
    w_main = jnp.concatenate([gl, fq, fk, fv, sq, sk, sv, dq, dk, dv], axis=1).astype(BF16)
    w_forget = jnp.pad(ff, ((0, 0), (0, LANES - FOX_HEADS))).astype(BF16)
    return w_main, w_forget


def _pack_router(r):
    p1 = r.astype(BF16)
    r1 = r - p1.astype(F32)
    p2 = r1.astype(BF16)
    p3 = (r1 - p2.astype(F32)).astype(BF16)
    packed = jnp.concatenate([p1, p2, p3], axis=1)
    return jnp.pad(packed, ((0, 0), (0, LANES - 3 * N_EXPERTS)))


def kernel(x, c, w_ada, b_ada, norm_g, w_in, fox_f_bias, diff_lambda, diff_subln_g, w_branch,
           w_out, ffn_w_gate, ffn_w_up, ffn_w_down, moe_router, moe_w_gate, moe_w_up,
           moe_w_down):
    batch, seq, _ = x.shape
    depth = w_in.shape[0]
    tokens = batch * seq
    x2d = x.reshape(tokens, D_MODEL)
    mod = _ada_modulation(c, w_ada, b_ada)
    slopes = jnp.exp2(-ALIBI_MAX_BIAS * jnp.arange(1, DIFF_HEADS + 1, dtype=F32) / DIFF_HEADS)
    slopes = jnp.broadcast_to(slopes[:, None, None], (DIFF_HEADS, 1, LANES))

    for layer in range(depth):
        lam_init = 0.8 - 0.6 * math.exp(-0.3 * layer)
        mod_attn, mod_ffn = 2 * layer, 2 * layer + 1
        w_main, w_forget = _prep_w_in(w_in[layer])
        proj, forget_logits = _in_projection(
            x2d, norm_g[layer, 0][None, :], mod, mod_attn, w_main, w_forget, seq)
        key_bias = _fox_decay(forget_logits, fox_f_bias[layer], batch, seq)

        o_fox = _attention_call(
            _fox_kernel, proj, (COLBLK_FQ, COLBLK_FK, COLBLK_FV), [key_bias],
            [pl.BlockSpec((seq, LANES), lambda b, c: (b, c))],
            [_values_t_scratch(seq), _values_t_scratch(seq)] + _pipeline_scratch(1),
            batch, seq, FOX_HEADS // 2, "fox_attention")
        o_sb = _attention_call(
            _sb_kernel, proj, (COLBLK_SQ, COLBLK_SK, COLBLK_SV), [], [],
            [_values_t_scratch(seq), _values_t_scratch(seq)] + _pipeline_scratch(1),
            batch, seq, SB_HEADS // 2, "stickbreak_attention")
        o_diff = _attention_call(
            functools.partial(_diff_kernel, lam_init=lam_init), proj,
            (COLBLK_DQ, COLBLK_DK, COLBLK_DV),
            [slopes, diff_lambda[layer].astype(F32), diff_subln_g[layer][None, :].astype(F32)],
            [pl.BlockSpec((None, 1, LANES), lambda b, c: (c, 0, 0)),
             pl.BlockSpec((4, HEAD_DIM), lambda b, c: (0, 0)),
             pl.BlockSpec((1, 2 * HEAD_DIM), lambda b, c: (0, 0))],
            [_values_t_scratch(seq), pltpu.VMEM((seq, LANES), BF16)] + _pipeline_scratch(2),
            batch, seq, DIFF_HEADS, "diff_attention")

        is_moe = layer % 2 == 1
        idx = layer // 2
        router_packed = _pack_router(moe_router[idx]) if is_moe else None
        merged = _merge(
            o_fox, o_sb, o_diff, proj, x2d, w_branch[layer].astype(BF16),
            w_out[layer].astype(BF16), norm_g[layer, 1][None, :], norm_g[layer, 2][None, :],
            mod, mod_attn, mod_ffn, router_packed, seq)
        g3 = norm_g[layer, 3][None, :]
        if is_moe:
            x2d, h, combine = merged
            x2d = _moe_ffn(h, combine, x2d, moe_w_gate[idx].astype(BF16),
                           moe_w_up[idx].astype(BF16), moe_w_down[idx].astype(BF16),
                           g3, mod, mod_ffn, seq)
        else:
            x2d, h = merged
            x2d = _dense_ffn(h, x2d, ffn_w_gate[idx].astype(BF16), ffn_w_up[idx].astype(BF16),
                             ffn_w_down[idx].astype(BF16), g3, mod, mod_ffn, seq)
    return x2d.reshape(batch, seq, D_MODEL)
```

```python
import functools
import math

import jax
import jax.numpy as jnp
import numpy as np
from jax import lax
from jax.experimental import pallas as pl
from jax.experimental.pallas import tpu as pltpu

F32 = jnp.float32
BF16 = jnp.bfloat16

D_MODEL = 1024
HEAD_DIM = 64
FOX_HEADS = 8
SB_HEADS = 8
DIFF_HEADS = 4
BRANCH_WIDTH = 512
N_BRANCHES = 3
FFN_DIM = 3584
N_EXPERTS = 8
RMS_EPS = 1e-6
ALIBI_MAX_BIAS = 8.0

LANES = 128
GATE_COLS = N_BRANCHES * D_MODEL
COLBLK_FQ = GATE_COLS // LANES
COLBLK_FK = COLBLK_FQ + 4
COLBLK_FV = COLBLK_FK + 4
COLBLK_SQ = COLBLK_FV + 4
COLBLK_SK = COLBLK_SQ + 4
COLBLK_SV = COLBLK_SK + 4
COLBLK_DQ = COLBLK_SV + 4
COLBLK_DK = COLBLK_DQ + 4
COLBLK_DV = COLBLK_DK + 4
PROJ_COLS = (COLBLK_DV + 4) * LANES

VMEM_LIMIT = 56 * 1024 * 1024
NEG_BIG = -1e30
QK_SCALE = HEAD_DIM ** -0.5


def _cparams(sem):
    return pltpu.CompilerParams(dimension_semantics=sem, vmem_limit_bytes=VMEM_LIMIT)


def _sigmoid(v):
    return 1.0 / (1.0 + jnp.exp(-v))


def _rms(v, g):
    ms = jnp.mean(v * v, axis=-1, keepdims=True)
    return v * lax.rsqrt(ms + RMS_EPS) * g


def _ada_kernel(c_ref, w_ref, b_ref, o_ref):
    c = c_ref[...]
    a = (c * _sigmoid(c)).astype(BF16)
    o_ref[...] = jnp.dot(a, w_ref[...].astype(BF16), preferred_element_type=F32) + b_ref[...]


def _ada_modulation(c, w_ada, b_ada):
    n_mod = w_ada.shape[0] * w_ada.shape[1]
    batch = c.shape[0]
    w = w_ada.reshape(n_mod, D_MODEL, 3 * D_MODEL)
    b = b_ada.reshape(n_mod, 1, 3 * D_MODEL)
    out = pl.pallas_call(
        _ada_kernel,
        grid=(n_mod, 3),
        in_specs=[
            pl.BlockSpec((batch, D_MODEL), lambda m, j: (0, 0)),
            pl.BlockSpec((None, D_MODEL, D_MODEL), lambda m, j: (m, 0, j)),
            pl.BlockSpec((None, 1, D_MODEL), lambda m, j: (m, 0, j)),
        ],
        out_specs=pl.BlockSpec((None, batch, D_MODEL), lambda m, j: (m, 0, j)),
        out_shape=jax.ShapeDtypeStruct((n_mod, batch, 3 * D_MODEL), F32),
        compiler_params=_cparams(("parallel", "parallel")),
        name="ada_modulation",
    )(c, w, b)
    return out.reshape(n_mod, batch, 1, 3 * D_MODEL)


def _mod_spec(mod_idx, part, rows_per_batch_tiles):
    def index(i, *_):
        return (mod_idx, i // rows_per_batch_tiles, 0, part)
    return pl.BlockSpec((None, None, 1, D_MODEL), index)


def _inproj_kernel(x_ref, g_ref, sc_ref, sh_ref, w_ref, wf_ref, o_ref, f_ref, h_scr):
    @pl.when(pl.program_id(1) == 0)
    def _():
        h = _rms(x_ref[...], g_ref[...]) * (1.0 + sc_ref[...]) + sh_ref[...]
        hb = h.astype(BF16)
        h_scr[...] = hb
        f_ref[...] = jnp.dot(hb, wf_ref[...], preferred_element_type=F32)

    o_ref[...] = jnp.dot(h_scr[...], w_ref[...], preferred_element_type=F32).astype(BF16)


def _in_projection(x2d, g, mod, mod_idx, w_main, w_forget, seq):
    tokens = x2d.shape[0]
    tm = min(1024, seq)
    tn = 1536
    tiles_per_batch = seq // tm
    return pl.pallas_call(
        _inproj_kernel,
        grid=(tokens // tm, PROJ_COLS // tn),
        in_specs=[
            pl.BlockSpec((tm, D_MODEL), lambda i, j: (i, 0)),
            pl.BlockSpec((1, D_MODEL), lambda i, j: (0, 0)),
            _mod_spec(mod_idx, 1, tiles_per_batch),
            _mod_spec(mod_idx, 0, tiles_per_batch),
            pl.BlockSpec((D_MODEL, tn), lambda i, j: (0, j)),
            pl.BlockSpec((D_MODEL, LANES), lambda i, j: (0, 0)),
        ],
        out_specs=[
            pl.BlockSpec((tm, tn), lambda i, j: (i, j)),
            pl.BlockSpec((tm, LANES), lambda i, j: (i, 0)),
        ],
        out_shape=[
            jax.ShapeDtypeStruct((tokens, PROJ_COLS), BF16),
            jax.ShapeDtypeStruct((tokens, LANES), F32),
        ],
        scratch_shapes=[pltpu.VMEM((tm, D_MODEL), BF16)],
        compiler_params=_cparams(("parallel", "arbitrary")),
        name="in_projection",
    )(x2d, g, mod, mod, w_main, w_forget)


CUMSUM_BLOCK = 256
BIAS_PIECES = 3


def _split3(v):
    p1 = v.astype(BF16).astype(F32)
    r1 = v - p1
    p2 = r1.astype(BF16).astype(F32)
    p3 = (r1 - p2).astype(BF16).astype(F32)
    return p1, p2, p3


def _decay_kernel(f_ref, b_ref, sel_ref, o_ref):
    seq = f_ref.shape[0]
    row = lax.broadcasted_iota(jnp.int32, (CUMSUM_BLOCK, CUMSUM_BLOCK), 0)
    col = lax.broadcasted_iota(jnp.int32, (CUMSUM_BLOCK, CUMSUM_BLOCK), 1)
    tri = jnp.where(col <= row, 1.0, 0.0).astype(BF16)
    carry = jnp.zeros((1, LANES), F32)
    for blk in range(seq // CUMSUM_BLOCK):
        rows = slice(blk * CUMSUM_BLOCK, (blk + 1) * CUMSUM_BLOCK)
        z = f_ref[rows, :] + b_ref[...]
        logf = jnp.minimum(z, 0.0) - jnp.log1p(jnp.exp(-jnp.abs(z)))
        cum = carry
        for piece in _split3(logf):
            cum = cum + jnp.dot(tri, piece.astype(BF16), preferred_element_type=F32)
        carry = cum[CUMSUM_BLOCK - 1:CUMSUM_BLOCK, :]
        bias = None
        for i, piece in enumerate(_split3(-cum)):
            part = jnp.dot(piece.astype(BF16), sel_ref[i], preferred_element_type=F32)
            bias = part if bias is None else bias + part
        o_ref[rows, :] = bias.astype(BF16)


def _decay_selectors():
    sel = np.zeros((BIAS_PIECES, LANES, FOX_HEADS // 2 * LANES), np.float32)
    for h in range(FOX_HEADS):
        for i in range(BIAS_PIECES):
            sel[i, h, (h // 2) * LANES + (h % 2) * BIAS_PIECES + i] = 1.0
    return jnp.asarray(sel, BF16)


def _fox_decay(forget_logits, bias, batch, seq):
    bias_row = jnp.zeros((1, LANES), F32).at[0, :FOX_HEADS].set(bias.astype(F32))
    width = FOX_HEADS // 2 * LANES
    return pl.pallas_call(
        _decay_kernel,
        grid=(batch,),
        in_specs=[
            pl.BlockSpec((seq, LANES), lambda b: (b, 0)),
            pl.BlockSpec((1, LANES), lambda b: (0, 0)),
            pl.BlockSpec((BIAS_PIECES, LANES, width), lambda b: (0, 0, 0)),
        ],
        out_specs=pl.BlockSpec((seq, width), lambda b: (b, 0)),
        out_shape=jax.ShapeDtypeStruct((batch * seq, width), BF16),
        compiler_params=_cparams(("parallel",)),
        name="fox_decay",
    )(forget_logits, bias_row, _decay_selectors())


ATT_BLOCK = 256


def _transposed_queries(q):
    return q.astype(F32).T.astype(BF16)


def _head_halves(x, axis):
    low = lax.broadcasted_iota(jnp.int32, x.shape, axis) < HEAD_DIM
    zero = jnp.zeros_like(x)
    return jnp.where(low, x, zero), jnp.where(low, zero, x)


def _key_query_masks(t):
    key = lax.broadcasted_iota(jnp.int32, (t, t), 0)
    query = lax.broadcasted_iota(jnp.int32, (t, t), 1)
    return key <= query, key < query


def _bias_ones(first_lane):
    lane = lax.broadcasted_iota(jnp.int32, (ATT_BLOCK, LANES), 1)
    hit = (lane >= first_lane) & (lane < first_lane + BIAS_PIECES)
    return jnp.where(hit, 1.0, 0.0).astype(BF16)


def _store_values_t(v_ref, vt_refs):
    t = ATT_BLOCK
    for blk in range(v_ref.shape[0] // t):
        vt = v_ref[blk * t:(blk + 1) * t, :].astype(F32).T
        if len(vt_refs) == 2:
            va, vb = _head_halves(vt, 0)
            vt_refs[0][blk] = va.astype(BF16)
            vt_refs[1][blk] = vb.astype(BF16)
        else:
            vt_refs[0][blk] = vt.astype(BF16)


def _softmax_step_t(s, m, l):
    m_new = jnp.maximum(m, jnp.max(s, axis=0, keepdims=True))
    alpha = jnp.exp(m - m_new)
    p = jnp.exp(s - m_new)
    l = alpha * l + jnp.sum(p, axis=0, keepdims=True)
    return m_new, l, alpha, p.astype(BF16)


def _stat_init(t):
    return jnp.full((1, t), NEG_BIG, F32), jnp.zeros((1, t), F32)


def _fox_kernel(q_ref, k_ref, v_ref, kb_ref, o_ref, vta_ref, vtb_ref, s_scr, p_scr, acc_scr):
    t = ATT_BLOCK
    n_blocks = q_ref.shape[0] // t
    _store_values_t(v_ref, (vta_ref, vtb_ref))
    causal, _ = _key_query_masks(t)
    ones_a, ones_b = _bias_ones(0), _bias_ones(BIAS_PIECES)
    head_a_rows = lax.broadcasted_iota(jnp.int32, (LANES, t), 0) < HEAD_DIM

    def q_block(qi, _):
        q0 = pl.multiple_of(qi * t, t)
        qa, qb = _head_halves(q_ref[pl.ds(q0, t), :] * QK_SCALE, 1)
        qta = _transposed_queries(jnp.concatenate([qa, ones_a], axis=1))
        qtb = _transposed_queries(jnp.concatenate([qb, ones_b], axis=1))

        def scores(j):
            k0 = pl.multiple_of(j * t, t)
            k = jnp.concatenate([k_ref[pl.ds(k0, t), :], kb_ref[pl.ds(k0, t), :]], axis=1)
            return (jnp.dot(k, qta, preferred_element_type=F32),
                    jnp.dot(k, qtb, preferred_element_type=F32))

        def accumulate(j, pa, pb, alpha_a, alpha_b):
            acc_scr[...] = (jnp.where(head_a_rows, alpha_a, alpha_b) * acc_scr[...]
                            + jnp.dot(vta_ref[j], pa, preferred_element_type=F32)
                            + jnp.dot(vtb_ref[j], pb, preferred_element_type=F32))

        def stage(s_next, pa, pb):
            s_scr[0], s_scr[1] = s_next
            p_scr[0], p_scr[1] = pa, pb

        def trip(j, c):
            alpha_a, alpha_b, ma, la, mb, lb = c
            s_next = scores(j + 1)
            accumulate(jnp.maximum(j - 1, 0), p_scr[0], p_scr[1], alpha_a, alpha_b)
            ma, la, alpha_a, pa = _softmax_step_t(s_scr[0], ma, la)
            mb, lb, alpha_b, pb = _softmax_step_t(s_scr[1], mb, lb)
            stage(s_next, pa, pb)
            return alpha_a, alpha_b, ma, la, mb, lb

        no_p = jnp.zeros((t, t), BF16)
        one_row = jnp.ones((1, t), F32)
        stage(scores(0), no_p, no_p)
        acc_scr[...] = jnp.zeros_like(acc_scr)
        alpha_a, alpha_b, ma, la, mb, lb = lax.fori_loop(
            0, qi, trip, (one_row, one_row) + _stat_init(t) + _stat_init(t))
        accumulate(jnp.maximum(qi - 1, 0), p_scr[0], p_scr[1], alpha_a, alpha_b)
        _, la, alpha_a, pa = _softmax_step_t(jnp.where(causal, s_scr[0], NEG_BIG), ma, la)
        _, lb, alpha_b, pb = _softmax_step_t(jnp.where(causal, s_scr[1], NEG_BIG), mb, lb)
        accumulate(qi, pa, pb, alpha_a, alpha_b)
        o_t = acc_scr[...] / jnp.where(head_a_rows, la, lb)
        o_ref[pl.ds(q0, t), :] = o_t.T.astype(o_ref.dtype)
        return 0

    lax.fori_loop(0, n_blocks, q_block, 0)


def _sb_kernel(q_ref, k_ref, v_ref, o_ref, vta_ref, vtb_ref, z_scr, a_scr, acc_scr):
    t = ATT_BLOCK
    n_blocks = q_ref.shape[0] // t
    _store_values_t(v_ref, (vta_ref, vtb_ref))
    _, strict = _key_query_masks(t)
    row = lax.broadcasted_iota(jnp.int32, (t, t), 0)
    col = lax.broadcasted_iota(jnp.int32, (t, t), 1)
    later_keys = jnp.where(col > row, 1.0, 0.0).astype(BF16)

    def q_block(qi, _):
        q0 = pl.multiple_of(qi * t, t)
        qa, qb = _head_halves(q_ref[pl.ds(q0, t), :] * QK_SCALE, 1)
        qta, qtb = _transposed_queries(qa), _transposed_queries(qb)

        def scores(step):
            j = jnp.maximum(qi - step, 0)
            k = k_ref[pl.ds(pl.multiple_of(j * t, t), t), :]
            return (jnp.dot(k, qta, preferred_element_type=F32),
                    jnp.dot(k, qtb, preferred_element_type=F32))

        def weights(z, later, diagonal):
            softplus = jnp.maximum(z, 0.0) + jnp.log(1.0 + jnp.exp(-jnp.abs(z)))
            log_stay = -softplus
            if diagonal:
                log_stay = jnp.where(strict, log_stay, 0.0)
            tail = jnp.dot(later_keys, log_stay.astype(BF16),
                           preferred_element_type=F32) + later
            a = jnp.exp(z + log_stay + tail)
            if diagonal:
                a = jnp.where(strict, a, 0.0)
            return a.astype(BF16), later + jnp.sum(log_stay, axis=0, keepdims=True)

        def accumulate(step, aa, ab):
            j = qi - step
            acc_scr[...] += (jnp.dot(vta_ref[j], aa, preferred_element_type=F32)
                             + jnp.dot(vtb_ref[j], ab, preferred_element_type=F32))

        def stage(z_next, aa, ab):
            z_scr[0], z_scr[1] = z_next
            a_scr[0], a_scr[1] = aa, ab

        def trip(step, c):
            later_a, later_b = c
            z_next = scores(step + 1)
            accumulate(step - 1, a_scr[0], a_scr[1])
            aa, later_a = weights(z_scr[0], later_a, False)
            ab, later_b = weights(z_scr[1], later_b, False)
            stage(z_next, aa, ab)
            return later_a, later_b

        za, zb = scores(0)
        z_next = scores(1)
        zero_row = jnp.zeros((1, t), F32)
        aa, later_a = weights(za, zero_row, True)
        ab, later_b = weights(zb, zero_row, True)
        stage(z_next, aa, ab)
        acc_scr[...] = jnp.zeros_like(acc_scr)
        lax.fori_loop(1, qi + 1, trip, (later_a, later_b))
        accumulate(qi, a_scr[0], a_scr[1])
        o_ref[pl.ds(q0, t), :] = acc_scr[...].T.astype(o_ref.dtype)
        return 0

    lax.fori_loop(0, n_blocks, q_block, 0)


def _diff_kernel(q_ref, k_ref, v_ref, slope_ref, dl_ref, g_ref, o_ref, vt_ref, kb_ref,
                 s_scr, p_scr, acc_scr, *, lam_init):
    t = ATT_BLOCK
    n_blocks = q_ref.shape[0] // t
    _store_values_t(v_ref, (vt_ref,))
    slope = slope_ref[:, 0:1]
    lane = lax.broadcasted_iota(jnp.int32, (t, LANES), 1)
    for blk in range(n_blocks):
        pos = (lax.broadcasted_iota(jnp.int32, (t, LANES), 0) + blk * t).astype(F32)
        p1, p2, p3 = _split3(slope * pos)
        bias = jnp.where(lane == 0, p1, jnp.where(lane == 1, p2, jnp.where(lane == 2, p3, 0.0)))
        kb_ref[blk * t:(blk + 1) * t, :] = bias.astype(BF16)
    causal, _ = _key_query_masks(t)
    ones = _bias_ones(0)
    dl = dl_ref[...]
    lam = (jnp.exp(jnp.sum(dl[0:1] * dl[1:2], axis=1, keepdims=True))
           - jnp.exp(jnp.sum(dl[2:3] * dl[3:4], axis=1, keepdims=True)) + lam_init)

    def q_block(qi, _):
        q0 = pl.multiple_of(qi * t, t)
        q1, q2 = _head_halves(q_ref[pl.ds(q0, t), :] * QK_SCALE, 1)
        qt1 = _transposed_queries(jnp.concatenate([q1, ones], axis=1))
        qt2 = _transposed_queries(jnp.concatenate([q2, ones], axis=1))

        def scores(j):
            k0 = pl.multiple_of(j * t, t)
            k = jnp.concatenate([k_ref[pl.ds(k0, t), :], kb_ref[pl.ds(k0, t), :]], axis=1)
            return (jnp.dot(k, qt1, preferred_element_type=F32),
                    jnp.dot(k, qt2, preferred_element_type=F32))

        def accumulate(j, p1, p2, alpha1, alpha2):
            vt = vt_ref[j]
            acc_scr[0] = alpha1 * acc_scr[0] + jnp.dot(vt, p1, preferred_element_type=F32)
            acc_scr[1] = alpha2 * acc_scr[1] + jnp.dot(vt, p2, preferred_element_type=F32)

        def stage(s_next, p1, p2):
            s_scr[0], s_scr[1] = s_next
            p_scr[0], p_scr[1] = p1, p2

        def trip(j, c):
            alpha1, alpha2, m1, l1, m2, l2 = c
            s_next = scores(j + 1)
            accumulate(jnp.maximum(j - 1, 0), p_scr[0], p_scr[1], alpha1, alpha2)
            m1, l1, alpha1, p1 = _softmax_step_t(s_scr[0], m1, l1)
            m2, l2, alpha2, p2 = _softmax_step_t(s_scr[1], m2, l2)
            stage(s_next, p1, p2)
            return alpha1, alpha2, m1, l1, m2, l2

        no_p = jnp.zeros((t, t), BF16)
        one_row = jnp.ones((1, t), F32)
        stage(scores(0), no_p, no_p)
        acc_scr[...] = jnp.zeros_like(acc_scr)
        alpha1, alpha2, m1, l1, m2, l2 = lax.fori_loop(
            0, qi, trip, (one_row, one_row) + _stat_init(t) + _stat_init(t))
        accumulate(jnp.maximum(qi - 1, 0), p_scr[0], p_scr[1], alpha1, alpha2)
        _, l1, alpha1, p1 = _softmax_step_t(jnp.where(causal, s_scr[0], NEG_BIG), m1, l1)
        _, l2, alpha2, p2 = _softmax_step_t(jnp.where(causal, s_scr[1], NEG_BIG), m2, l2)
        accumulate(qi, p1, p2, alpha1, alpha2)
        o = (acc_scr[0] / l1 - lam * (acc_scr[1] / l2)).T
        o_ref[pl.ds(q0, t), :] = (_rms(o, g_ref[...]) * (1.0 - lam_init)).astype(o_ref.dtype)
        return 0

    lax.fori_loop(0, n_blocks, q_block, 0)


def _attention_call(kernel, proj, colblks, extra_in, extra_specs, scratch, batch, seq,
                    n_colblk, name):
    qc, kc, vc = colblks
    return pl.pallas_call(
        kernel,
        grid=(batch, n_colblk),
        in_specs=[
            pl.BlockSpec((seq, LANES), lambda b, c: (b, qc + c)),
            pl.BlockSpec((seq, LANES), lambda b, c: (b, kc + c)),
            pl.BlockSpec((seq, LANES), lambda b, c: (b, vc + c)),
        ] + extra_specs,
        out_specs=pl.BlockSpec((seq, LANES), lambda b, c: (b, c)),
        out_shape=jax.ShapeDtypeStruct((batch * seq, n_colblk * LANES), BF16),
        scratch_shapes=scratch,
        compiler_params=_cparams(("parallel", "parallel")),
        name=name,
    )(proj, proj, proj, *extra_in)


def _values_t_scratch(seq):
    return pltpu.VMEM((seq // ATT_BLOCK, LANES, ATT_BLOCK), BF16)


def _pipeline_scratch(n_acc):
    acc_shape = (LANES, ATT_BLOCK) if n_acc == 1 else (n_acc, LANES, ATT_BLOCK)
    return [pltpu.VMEM((2, ATT_BLOCK, ATT_BLOCK), F32),
            pltpu.VMEM((2, ATT_BLOCK, ATT_BLOCK), BF16),
            pltpu.VMEM(acc_shape, F32)]


ROUTE_EXPERT = 0
ROUTE_WEIGHT = 2


def _route(h, r_ref):
    h1 = h.astype(BF16)
    h2 = (h - h1.astype(F32)).astype(BF16)
    prod = (jnp.dot(h1, r_ref[...], preferred_element_type=F32)
            + jnp.dot(h2, r_ref[...], preferred_element_type=F32))
    lane = lax.broadcasted_iota(jnp.int32, prod.shape, 1)
    shifted1 = pltpu.roll(prod, LANES - N_EXPERTS, axis=1)
    shifted2 = pltpu.roll(prod, LANES - 2 * N_EXPERTS, axis=1)
    logits = jnp.where(lane < N_EXPERTS, prod + shifted1 + shifted2, NEG_BIG)
    v1 = jnp.max(logits, axis=1, keepdims=True)
    i1 = jnp.min(jnp.where(logits == v1, lane, LANES), axis=1, keepdims=True)
    rest = jnp.where(lane == i1, NEG_BIG, logits)
    v2 = jnp.max(rest, axis=1, keepdims=True)
    i2 = jnp.min(jnp.where(rest == v2, lane, LANES), axis=1, keepdims=True)
    e2 = jnp.exp(v2 - v1)
    w1 = 1.0 / (1.0 + e2)
    w2 = e2 / (1.0 + e2)
    record = jnp.where(lane == ROUTE_EXPERT, i1.astype(F32), 0.0)
    record = jnp.where(lane == ROUTE_EXPERT + 1, i2.astype(F32), record)
    record = jnp.where(lane == ROUTE_WEIGHT, w1, record)
    return jnp.where(lane == ROUTE_WEIGHT + 1, w2, record)


def _merge_kernel(*refs, with_router):
    if with_router:
        (of_ref, os_ref, od_ref, gl_ref, x_ref, wb_ref, wo_ref, g1_ref, gate_ref,
         g2_ref, sc_ref, sh_ref, r_ref, xo_ref, ho_ref, cw_ref) = refs
    else:
        (of_ref, os_ref, od_ref, gl_ref, x_ref, wb_ref, wo_ref, g1_ref, gate_ref,
         g2_ref, sc_ref, sh_ref, xo_ref, ho_ref) = refs
    merged = None
    for n, o_ref in enumerate((of_ref, os_ref, od_ref)):
        p = jnp.dot(o_ref[...], wb_ref[n], preferred_element_type=F32)
        gated = _sigmoid(gl_ref[:, n * D_MODEL:(n + 1) * D_MODEL].astype(F32)) * p
        merged = gated if merged is None else merged + gated
    y = jnp.dot(merged.astype(BF16), wo_ref[...], preferred_element_type=F32)
    xn = x_ref[...] + gate_ref[...] * _rms(y, g1_ref[...])
    xo_ref[...] = xn
    h = _rms(xn, g2_ref[...]) * (1.0 + sc_ref[...]) + sh_ref[...]
    ho_ref[...] = h.astype(ho_ref.dtype)
    if with_router:
        cw_ref[...] = _route(h, r_ref)


def _merge(o_fox, o_sb, o_diff, proj, x2d, w_branch, w_out, g1, g2, mod, mod_attn, mod_ffn,
           router_packed, seq):
    tokens = x2d.shape[0]
    tm = min(512, seq)
    tpb = seq // tm
    with_router = router_packed is not None
    row = lambda i: (i, 0)
    const2 = lambda i: (0, 0)
    in_specs = [
        pl.BlockSpec((tm, BRANCH_WIDTH), row),
        pl.BlockSpec((tm, BRANCH_WIDTH), row),
        pl.BlockSpec((tm, BRANCH_WIDTH), row),
        pl.BlockSpec((tm, GATE_COLS), row),
        pl.BlockSpec((tm, D_MODEL), row),
        pl.BlockSpec((N_BRANCHES, BRANCH_WIDTH, D_MODEL), lambda i: (0, 0, 0)),
        pl.BlockSpec((D_MODEL, D_MODEL), const2),
        pl.BlockSpec((1, D_MODEL), const2),
        _mod_spec(mod_attn, 2, tpb),
        pl.BlockSpec((1, D_MODEL), const2),
        _mod_spec(mod_ffn, 1, tpb),
        _mod_spec(mod_ffn, 0, tpb),
    ]
    args = [o_fox, o_sb, o_diff, proj, x2d, w_branch, w_out, g1, mod, g2, mod, mod]
    out_specs = [pl.BlockSpec((tm, D_MODEL), row), pl.BlockSpec((tm, D_MODEL), row)]
    out_shape = [jax.ShapeDtypeStruct((tokens, D_MODEL), F32),
                 jax.ShapeDtypeStruct((tokens, D_MODEL), F32 if with_router else BF16)]
    if with_router:
        in_specs.append(pl.BlockSpec((D_MODEL, LANES), const2))
        args.append(router_packed)
        out_specs.append(pl.BlockSpec((tm, LANES), row))
        out_shape.append(jax.ShapeDtypeStruct((tokens, LANES), F32))
    return pl.pallas_call(
        functools.partial(_merge_kernel, with_router=with_router),
        grid=(tokens // tm,),
        in_specs=in_specs,
        out_specs=out_specs,
        out_shape=out_shape,
        compiler_params=_cparams(("parallel",)),
        name="merge_router" if with_router else "merge",
    )(*args)


def _swiglu_partial(h, wg, wu):
    gate = jnp.dot(h, wg, preferred_element_type=F32)
    up = jnp.dot(h, wu, preferred_element_type=F32)
    return gate * _sigmoid(gate) * up


def _ffn_kernel(h_ref, wg_ref, wu_ref, wd_ref, x_ref, g_ref, gate_ref, o_ref, acc_ref):
    j = pl.program_id(1)

    @pl.when(j == 0)
    def _():
        acc_ref[...] = jnp.zeros_like(acc_ref)

    act = _swiglu_partial(h_ref[...], wg_ref[...], wu_ref[...])
    acc_ref[...] += jnp.dot(act.astype(BF16), wd_ref[...], preferred_element_type=F32)

    @pl.when(j == pl.num_programs(1) - 1)
    def _():
        o_ref[...] = x_ref[...] + gate_ref[...] * _rms(acc_ref[...], g_ref[...])


def _dense_ffn(h, x2d, wg, wu, wd, g, mod, mod_idx, seq):
    tokens = x2d.shape[0]
    tm = min(1024, seq)
    tf = 512
    tpb = seq // tm
    return pl.pallas_call(
        _ffn_kernel,
        grid=(tokens // tm, FFN_DIM // tf),
        in_specs=[
            pl.BlockSpec((tm, D_MODEL), lambda i, j: (i, 0)),
            pl.BlockSpec((D_MODEL, tf), lambda i, j: (0, j)),
            pl.BlockSpec((D_MODEL, tf), lambda i, j: (0, j)),
            pl.BlockSpec((tf, D_MODEL), lambda i, j: (j, 0)),
            pl.BlockSpec((tm, D_MODEL), lambda i, j: (i, 0)),
            pl.BlockSpec((1, D_MODEL), lambda i, j: (0, 0)),
            _mod_spec(mod_idx, 2, tpb),
        ],
        out_specs=pl.BlockSpec((tm, D_MODEL), lambda i, j: (i, 0)),
        out_shape=jax.ShapeDtypeStruct((tokens, D_MODEL), F32),
        scratch_shapes=[pltpu.VMEM((tm, D_MODEL), F32)],
        compiler_params=_cparams(("parallel", "arbitrary")),
        name="dense_ffn",
    )(h, wg, wu, wd, x2d, g, mod)


MOE_TILE = 512
RANK_BLOCK = 512
ROW_DMA_UNROLL = 8


def _rank_kernel(route_ref, rank_ref, total_ref, count_scr):
    @pl.when(pl.program_id(0) == 0)
    def _():
        count_scr[...] = jnp.zeros_like(count_scr)

    route = route_ref[...]
    lane = lax.broadcasted_iota(jnp.int32, route.shape, 1)
    lane_f = lane.astype(F32)
    e1 = route[:, ROUTE_EXPERT:ROUTE_EXPERT + 1]
    e2 = route[:, ROUTE_EXPERT + 1:ROUTE_EXPERT + 2]
    member = jnp.where(lane_f == e1, 1.0, jnp.where(lane_f == e2, 1.0, 0.0))
    row = lax.broadcasted_iota(jnp.int32, (RANK_BLOCK, RANK_BLOCK), 0)
    col = lax.broadcasted_iota(jnp.int32, (RANK_BLOCK, RANK_BLOCK), 1)
    earlier = jnp.where(col < row, 1.0, 0.0).astype(BF16)
    before = jnp.dot(earlier, member.astype(BF16), preferred_element_type=F32) + count_scr[...]
    rank1 = jnp.sum(jnp.where(lane_f == e1, before, 0.0), axis=1, keepdims=True)
    rank2 = jnp.sum(jnp.where(lane_f == e2, before, 0.0), axis=1, keepdims=True)
    rank_ref[...] = jnp.where(lane == 0, rank1, jnp.where(lane == 1, rank2, 0.0))
    count_scr[...] += jnp.sum(member, axis=0, keepdims=True)
    total_ref[...] = count_scr[...]


def _expert_ranks(route):
    tokens = route.shape[0]
    return pl.pallas_call(
        _rank_kernel,
        grid=(tokens // RANK_BLOCK,),
        in_specs=[pl.BlockSpec((RANK_BLOCK, LANES), lambda i: (i, 0))],
        out_specs=[pl.BlockSpec((RANK_BLOCK, LANES), lambda i: (i, 0)),
                   pl.BlockSpec((1, LANES), lambda i: (0, 0))],
        out_shape=[jax.ShapeDtypeStruct((tokens, LANES), F32),
                   jax.ShapeDtypeStruct((1, LANES), F32)],
        scratch_shapes=[pltpu.VMEM((1, LANES), F32)],
        compiler_params=_cparams(("arbitrary",)),
        name="expert_ranks",
    )(route)


def _routing_tables(route, ranks, totals, n_tiles):
    counts = totals[0, :N_EXPERTS].astype(jnp.int32)
    padded = (counts + MOE_TILE - 1) // MOE_TILE * MOE_TILE
    ends = jnp.cumsum(padded)
    starts = ends - padded
    e1 = route[:, ROUTE_EXPERT].astype(jnp.int32)
    e2 = route[:, ROUTE_EXPERT + 1].astype(jnp.int32)
    pos1 = starts[e1] + ranks[:, 0].astype(jnp.int32)
    pos2 = starts[e2] + ranks[:, 1].astype(jnp.int32)
    n_used = ends[-1:] // MOE_TILE
    tile_start = jnp.minimum(jnp.arange(n_tiles, dtype=jnp.int32), n_used - 1) * MOE_TILE
    tile_expert = jnp.sum(tile_start[:, None] >= ends[None, :], axis=1).astype(jnp.int32)
    return pos1, pos2, tile_expert, n_used.astype(jnp.int32)


def _dispatch_kernel(pos1_ref, pos2_ref, h_ref, init_hbm, xs_hbm, sem, *, tm):
    del init_hbm
    base = pl.program_id(0) * tm

    def row_copy(r, pos_ref):
        return pltpu.make_async_copy(
            h_ref.at[pl.ds(r, 1)], xs_hbm.at[pl.ds(pos_ref[base + r], 1)], sem)

    def issue(r, _):
        row_copy(r, pos1_ref).start()
        row_copy(r, pos2_ref).start()
        return 0

    def drain(r, _):
        row_copy(r, pos1_ref).wait()
        row_copy(r, pos2_ref).wait()
        return 0

    lax.fori_loop(0, tm, issue, 0, unroll=ROW_DMA_UNROLL)
    lax.fori_loop(0, tm, drain, 0, unroll=ROW_DMA_UNROLL)


def _dispatch(h, pos1, pos2, n_rows):
    tokens = h.shape[0]
    tm = 512
    return pl.pallas_call(
        functools.partial(_dispatch_kernel, tm=tm),
        grid_spec=pltpu.PrefetchScalarGridSpec(
            num_scalar_prefetch=2,
            grid=(tokens // tm,),
            in_specs=[pl.BlockSpec((tm, D_MODEL), lambda i, p1, p2: (i, 0)),
                      pl.BlockSpec(memory_space=pl.ANY)],
            out_specs=pl.BlockSpec(memory_space=pl.ANY),
            scratch_shapes=[pltpu.SemaphoreType.DMA],
        ),
        out_shape=jax.ShapeDtypeStruct((n_rows, D_MODEL), F32),
        input_output_aliases={3: 0},
        compiler_params=_cparams(("arbitrary",)),
        name="expert_dispatch",
    )(pos1, pos2, h, jnp.zeros((n_rows, D_MODEL), F32))


def _grouped_ffn_kernel(te_ref, nu_ref, xs_ref, wg_ref, wu_ref, wd_ref, ys_ref, xb_scr, acc_scr):
    del te_ref
    i = pl.program_id(0)
    j = pl.program_id(1)
    last = pl.num_programs(1) - 1
    used = i < nu_ref[0]

    @pl.when(used & (j == 0))
    def _():
        xb_scr[...] = xs_ref[...].astype(BF16)
        acc_scr[...] = jnp.zeros_like(acc_scr)

    @pl.when(used)
    def _():
        act = _swiglu_partial(xb_scr[...], wg_ref[...], wu_ref[...])
        acc_scr[...] += jnp.dot(act.astype(BF16), wd_ref[...], preferred_element_type=F32)

    @pl.when(used & (j == last))
    def _():
        ys_ref[...] = acc_scr[...]

    @pl.when(jnp.logical_not(used) & (j == last))
    def _():
        ys_ref[...] = jnp.zeros_like(ys_ref)


def _grouped_ffn(xs, tile_expert, n_used, wg, wu, wd):
    n_rows = xs.shape[0]
    tf = 1792
    nj = FFN_DIM // tf

    def ffn_col(i, j, nu):
        return jnp.where(i < nu[0], j, nj - 1)

    return pl.pallas_call(
        _grouped_ffn_kernel,
        grid_spec=pltpu.PrefetchScalarGridSpec(
            num_scalar_prefetch=2,
            grid=(n_rows // MOE_TILE, nj),
            in_specs=[
                pl.BlockSpec((MOE_TILE, D_MODEL),
                             lambda i, j, te, nu: (jnp.minimum(i, nu[0] - 1), 0)),
                pl.BlockSpec((None, D_MODEL, tf),
                             lambda i, j, te, nu: (te[i], 0, ffn_col(i, j, nu))),
                pl.BlockSpec((None, D_MODEL, tf),
                             lambda i, j, te, nu: (te[i], 0, ffn_col(i, j, nu))),
                pl.BlockSpec((None, tf, D_MODEL),
                             lambda i, j, te, nu: (te[i], ffn_col(i, j, nu), 0)),
            ],
            out_specs=pl.BlockSpec((MOE_TILE, D_MODEL), lambda i, j, te, nu: (i, 0)),
            scratch_shapes=[pltpu.VMEM((MOE_TILE, D_MODEL), BF16),
                            pltpu.VMEM((MOE_TILE, D_MODEL), F32)],
        ),
        out_shape=jax.ShapeDtypeStruct((n_rows, D_MODEL), F32),
        compiler_params=_cparams(("arbitrary", "arbitrary")),
        name="grouped_ffn",
    )(tile_expert, n_used, xs, wg, wu, wd)


def _combine_kernel(pos1_ref, pos2_ref, ys_hbm, route_ref, x_ref, g_ref, gate_ref, o_ref,
                    y1_scr, y2_scr, sem, *, tm):
    base = pl.program_id(0) * tm

    def row_copy(r, pos_ref, dst):
        return pltpu.make_async_copy(
            ys_hbm.at[pl.ds(pos_ref[base + r], 1)], dst.at[pl.ds(r, 1)], sem)

    def issue(r, _):
        row_copy(r, pos1_ref, y1_scr).start()
        row_copy(r, pos2_ref, y2_scr).start()
        return 0

    def drain(r, _):
        row_copy(r, pos1_ref, y1_scr).wait()
        row_copy(r, pos2_ref, y2_scr).wait()
        return 0

    lax.fori_loop(0, tm, issue, 0, unroll=ROW_DMA_UNROLL)
    lax.fori_loop(0, tm, drain, 0, unroll=ROW_DMA_UNROLL)
    route = route_ref[...]
    w1 = route[:, ROUTE_WEIGHT:ROUTE_WEIGHT + 1]
    w2 = route[:, ROUTE_WEIGHT + 1:ROUTE_WEIGHT + 2]
    y = w1 * y1_scr[...] + w2 * y2_scr[...]
    o_ref[...] = x_ref[...] + gate_ref[...] * _rms(y, g_ref[...])


def _combine(ys, pos1, pos2, route, x2d, g, mod, mod_idx, seq):
    tokens = x2d.shape[0]
    tm = min(256, seq)
    tpb = seq // tm
    gate_spec = _mod_spec(mod_idx, 2, tpb)
    return pl.pallas_call(
        functools.partial(_combine_kernel, tm=tm),
        grid_spec=pltpu.PrefetchScalarGridSpec(
            num_scalar_prefetch=2,
            grid=(tokens // tm,),
            in_specs=[
                pl.BlockSpec(memory_space=pl.ANY),
                pl.BlockSpec((tm, LANES), lambda i, p1, p2: (i, 0)),
                pl.BlockSpec((tm, D_MODEL), lambda i, p1, p2: (i, 0)),
                pl.BlockSpec((1, D_MODEL), lambda i, p1, p2: (0, 0)),
                pl.BlockSpec(gate_spec.block_shape, lambda i, p1, p2: gate_spec.index_map(i)),
            ],
            out_specs=pl.BlockSpec((tm, D_MODEL), lambda i, p1, p2: (i, 0)),
            scratch_shapes=[pltpu.VMEM((tm, D_MODEL), F32), pltpu.VMEM((tm, D_MODEL), F32),
                            pltpu.SemaphoreType.DMA],
        ),
        out_shape=jax.ShapeDtypeStruct((tokens, D_MODEL), F32),
        compiler_params=_cparams(("arbitrary",)),
        name="expert_combine",
    )(pos1, pos2, ys, route, x2d, g, mod)


def _moe_ffn(h, route, x2d, wg, wu, wd, g, mod, mod_idx, seq):
    tokens = x2d.shape[0]
    n_rows = 2 * tokens + N_EXPERTS * MOE_TILE
    ranks, totals = _expert_ranks(route)
    pos1, pos2, tile_expert, n_used = _routing_tables(route, ranks, totals, n_rows // MOE_TILE)
    xs = _dispatch(h, pos1, pos2, n_rows)
    ys = _grouped_ffn(xs, tile_expert, n_used, wg, wu, wd)
    return _combine(ys, pos1, pos2, route, x2d, g, mod, mod_idx, seq)


def _prep_w_in(w):
    fw = FOX_HEADS * HEAD_DIM
    sw = SB_HEADS * HEAD_DIM
    dw = DIFF_HEADS * 2 * HEAD_DIM
    sizes = (fw, fw, fw, FOX_HEADS, sw, sw, sw, dw, dw, dw, GATE_COLS)
    offs = [0]
    for s in sizes:
        offs.append(offs[-1] + s)
    part = [w[:, offs[i]:offs[i + 1]] for i in range(len(sizes))]
    fq, fk, fv, ff, sq, sk, sv, dq, dk, dv, gl = part
    w_main = jnp.concatenate([gl, fq, fk, fv, sq, sk, sv, dq, dk, dv], axis=1).astype(BF16)
    w_forget = jnp.pad(ff, ((0, 0), (0, LANES - FOX_HEADS))).astype(BF16)
    return w_main, w_forget


def _pack_router(r):
    p1 = r.astype(BF16)
    r1 = r - p1.astype(F32)
    p2 = r1.astype(BF16)
    p3 = (r1 - p2.astype(F32)).astype(BF16)
    packed = jnp.concatenate([p1, p2, p3], axis=1)
    return jnp.pad(packed, ((0, 0), (0, LANES - 3 * N_EXPERTS)))


def kernel(x, c, w_ada, b_ada, norm_g, w_in, fox_f_bias, diff_lambda, diff_subln_g, w_branch,
           w_out, ffn_w_gate, ffn_w_up, ffn_w_down, moe_router, moe_w_gate, moe_w_up,
           moe_w_down):
    batch, seq, _ = x.shape
    depth = w_in.shape[0]
    tokens = batch * seq
    x2d = x.reshape(tokens, D_MODEL)
    mod = _ada_modulation(c, w_ada, b_ada)
    slopes = jnp.exp2(-ALIBI_MAX_BIAS * jnp.arange(1, DIFF_HEADS + 1, dtype=F32) / DIFF_HEADS)
    slopes = jnp.broadcast_to(slopes[:, None, None], (DIFF_HEADS, 1, LANES))

    for layer in range(depth):
        lam_init = 0.8 - 0.6 * math.exp(-0.3 * layer)
        mod_attn, mod_ffn = 2 * layer, 2 * layer + 1
        w_main, w_forget = _prep_w_in(w_in[layer])
        proj, forget_logits = _in_projection(
            x2d, norm_g[layer, 0][None, :], mod, mod_attn, w_main, w_forget, seq)
        key_bias = _fox_decay(forget_logits, fox_f_bias[layer], batch, seq)

        o_fox = _attention_call(
            _fox_kernel, proj, (COLBLK_FQ, COLBLK_FK, COLBLK_FV), [key_bias],
            [pl.BlockSpec((seq, LANES), lambda b, c: (b, c))],
            [_values_t_scratch(seq), _values_t_scratch(seq)] + _pipeline_scratch(1),
            batch, seq, FOX_HEADS // 2, "fox_attention")
        o_sb = _attention_call(
            _sb_kernel, proj, (COLBLK_SQ, COLBLK_SK, COLBLK_SV), [], [],
            [_values_t_scratch(seq), _values_t_scratch(seq)] + _pipeline_scratch(1),
            batch, seq, SB_HEADS // 2, "stickbreak_attention")
        o_diff = _attention_call(
            functools.partial(_diff_kernel, lam_init=lam_init), proj,
            (COLBLK_DQ, COLBLK_DK, COLBLK_DV),
            [slopes, diff_lambda[layer].astype(F32), diff_subln_g[layer][None, :].astype(F32)],
            [pl.BlockSpec((None, 1, LANES), lambda b, c: (c, 0, 0)),
             pl.BlockSpec((4, HEAD_DIM), lambda b, c: (0, 0)),
             pl.BlockSpec((1, 2 * HEAD_DIM), lambda b, c: (0, 0))],
            [_values_t_scratch(seq), pltpu.VMEM((seq, LANES), BF16)] + _pipeline_scratch(2),
            batch, seq, DIFF_HEADS, "diff_attention")

        is_moe = layer % 2 == 1
        idx = layer // 2
        router_packed = _pack_router(moe_router[idx]) if is_moe else None
        merged = _merge(
            o_fox, o_sb, o_diff, proj, x2d, w_branch[layer].astype(BF16),
            w_out[layer].astype(BF16), norm_g[layer, 1][None, :], norm_g[layer, 2][None, :],
            mod, mod_attn, mod_ffn, router_packed, seq)
        g3 = norm_g[layer, 3][None, :]
        if is_moe:
            x2d, h, combine = merged
            x2d = _moe_ffn(h, combine, x2d, moe_w_gate[idx].astype(BF16),
                           moe_w_up[idx].astype(BF16), moe_w_down[idx].astype(BF16),
                           g3, mod, mod_ffn, seq)
        else:
            x2d, h = merged
            x2d = _dense_ffn(h, x2d, ffn_w_gate[idx].astype(BF16), ffn_w_up[idx].astype(BF16),
                             ffn_w_down[idx].astype(BF16), g3, mod, mod_ffn, seq)
    return x2d.reshape(batch, seq, D_MODEL)
```
